```python
import jax, jax.numpy as jnp
from jax import lax
import numpy as np

D_MODEL = 1024
BATCH = 4
SEQ = 8192
DEPTH = 1
DEC_BATCH = 128
DEC_SEQ = 8
PAST_LEN = 8192
PAGE_SIZE = 128

D_MIX = D_MODEL
A_WIDTH = D_MIX // 2
A_GROUPS = 4
A_GROUP_DIM = A_WIDTH // A_GROUPS
CHUNK = 128
B_WIDTH = D_MIX - A_WIDTH
B_HEADS = 8
B_HEAD_DIM = B_WIDTH // B_HEADS
ROT_DIM = B_HEAD_DIM // 4
ROPE_THETA = 500000.0
DILATED_CONFIGS = ((128, 1), (512, 4), (2048, 16))
WIN_MAX = 2048
MAX_DIL = 16
BAND = 128
EPS = 1e-6
IN_COLS = 3 * A_WIDTH + 4 * B_WIDTH

kernel_name = "hymba_gmlp_dilated_attn_step"


def _rmsnorm(x, g):
    xf = x.astype(jnp.float32)
    y = xf * lax.rsqrt(jnp.mean(xf * xf, axis=-1, keepdims=True) + EPS)
    return (y * g.astype(jnp.float32)).astype(x.dtype)


def _layernorm(x, g, b):
    xf = x.astype(jnp.float32)
    mu = jnp.mean(xf, axis=-1, keepdims=True)
    xc = xf - mu
    y = xc * lax.rsqrt(jnp.mean(xc * xc, axis=-1, keepdims=True) + EPS)
    return (y * g.astype(jnp.float32) + b.astype(jnp.float32)).astype(x.dtype)


def _rope_partial(x, pos):
    inv = ROPE_THETA ** (-jnp.arange(0, ROT_DIM, 2, dtype=jnp.float32) / ROT_DIM)
    ang = pos.astype(jnp.float32)[:, None] * inv[None, :]
    cos = jnp.cos(ang)[None, :, None, :]
    sin = jnp.sin(ang)[None, :, None, :]
    xf = x.astype(jnp.float32)
    x1 = xf[..., :ROT_DIM // 2]
    x2 = xf[..., ROT_DIM // 2:ROT_DIM]
    rot = jnp.concatenate([x1 * cos - x2 * sin, x1 * sin + x2 * cos], axis=-1).astype(x.dtype)
    return jnp.concatenate([rot, x[..., ROT_DIM:]], axis=-1)


def _project(x, pos, norm_g, w_in, ln_v_g, ln_v_b):
    h = _rmsnorm(x, norm_g)
    z = jnp.einsum('bsd,dc->bsc', h, w_in)
    u, va, ga, q, k, vb, gb = jnp.split(
        z, [A_WIDTH, 2 * A_WIDTH, 3 * A_WIDTH, 3 * A_WIDTH + B_WIDTH,
            3 * A_WIDTH + 2 * B_WIDTH, 3 * A_WIDTH + 3 * B_WIDTH], axis=-1)
    u = jax.nn.gelu(u)
    va = _layernorm(jax.nn.gelu(va), ln_v_g, ln_v_b)
    Bn, S, _ = x.shape
    q = _rope_partial(q.reshape(Bn, S, B_HEADS, B_HEAD_DIM), pos)
    k = _rope_partial(k.reshape(Bn, S, B_HEADS, B_HEAD_DIM), pos)
    vb = vb.reshape(Bn, S, B_HEADS, B_HEAD_DIM)
    return u, va, ga, q, k, vb, gb


def _spatial_gate(u, va, w_spatial, b_spatial):
    n = va.shape[2]
    w = w_spatial[:, :n, :n] * jnp.tril(jnp.ones((n, n), w_spatial.dtype))
    s = jnp.einsum('gij,bcjgf->bcigf', w, va) + b_spatial[:, :n].T[None, None, :, :, None]
    return u * s


def _dilated_branch_prompt(q, k, v, dil, n_strides):
    Bn, Sp, H, E = q.shape
    n = Sp // dil
    nb = n // BAND

    def blocks(x):
        return x.reshape(Bn, n, dil, H, E).transpose(0, 2, 3, 1, 4).reshape(Bn, dil, H, nb, BAND, E)

    def with_prev(x):
        prev = jnp.pad(x, ((0, 0), (0, 0), (0, 0), (1, 0), (0, 0), (0, 0)))[:, :, :, :-1]
        return jnp.concatenate([prev, x], axis=4)

    qb = blocks(q)
    kc = with_prev(blocks(k))
    vc = with_prev(blocks(v))
    s = jnp.einsum('bdhnqe,bdhnke->bdhnqk', qb, kc).astype(jnp.float32) * (E ** -0.5)
    qi = jnp.arange(BAND)[:, None]
    kk = jnp.arange(2 * BAND)[None, :]
    dist = qi + BAND - kk
    valid = (dist >= 0) & (dist <= n_strides)
    has_prev = jnp.arange(nb)[:, None, None] > 0
    mask = valid[None] & (has_prev | (kk >= BAND)[None])
    s = jnp.where(mask, s, -jnp.inf)
    m = jnp.max(s, axis=-1, keepdims=True)
    p = jnp.exp(s - m)
    den = jnp.sum(p, axis=-1)
    o = jnp.einsum('bdhnqk,bdhnke->bdhnqe', p, vc.astype(jnp.float32)) / den[..., None]
    lse = m[..., 0] + jnp.log(den)
    o = o.reshape(Bn, dil, H, n, E).transpose(0, 3, 1, 2, 4).reshape(Bn, Sp, H, E)
    lse = lse.reshape(Bn, dil, H, n).transpose(0, 3, 1, 2).reshape(Bn, Sp, H)
    return o, lse


def _dilated_branch_sample(q, k_all, v_all, dil, n_strides):
    T = q.shape[1]
    E = q.shape[-1]
    w_buf = k_all.shape[1] - T
    idx = w_buf + jnp.arange(T)[:, None] - dil * jnp.arange(n_strides + 1)[None, :]
    valid = idx >= 0
    idx = jnp.maximum(idx, 0)
    kg = k_all[:, idx]
    vg = v_all[:, idx]
    s = jnp.einsum('bthe,btjhe->bthj', q, kg).astype(jnp.float32) * (E ** -0.5)
    s = jnp.where(valid[None, :, None, :], s, -jnp.inf)
    m = jnp.max(s, axis=-1, keepdims=True)
    p = jnp.exp(s - m)
    den = jnp.sum(p, axis=-1)
    o = jnp.einsum('bthj,btjhe->bthe', p, vg.astype(jnp.float32)) / den[..., None]
    lse = m[..., 0] + jnp.log(den)
    return o, lse


def _combine(outs, lses):
    w = jax.nn.softmax(jnp.stack(lses, axis=0), axis=0)
    return jnp.sum(w[..., None] * jnp.stack(outs, axis=0), axis=0)


def _finish(x, a_out, ga, b_out, gb, gn_a, gn_b, w_out, final_g):
    Bn, S, _ = x.shape
    b_out = b_out.reshape(Bn, S, B_WIDTH).astype(x.dtype)
    ha = _rmsnorm(a_out * jax.nn.silu(ga), gn_a)
    hb = _rmsnorm(b_out * jax.nn.silu(gb), gn_b)
    y = x + jnp.einsum('bsc,cd->bsd', jnp.concatenate([ha, hb], axis=-1), w_out)
    return _rmsnorm(y, final_g)


def setup_inputs(seed: int = 0) -> dict:
    key = jax.random.key(seed)
    ks = jax.random.split(key, 16)
    w_buf = min(WIN_MAX, PAST_LEN)
    nrm = jax.random.normal
    f32 = jnp.float32
    return {
        "x_prompt": nrm(ks[0], (BATCH, SEQ, D_MODEL), f32),
        "x_sample": nrm(ks[1], (DEC_BATCH, DEC_SEQ, D_MODEL), f32),
        "cache_k_win": nrm(ks[2], (DEC_BATCH, w_buf, B_HEADS, B_HEAD_DIM), f32),
        "cache_v_win": nrm(ks[3], (DEC_BATCH, w_buf, B_HEADS, B_HEAD_DIM), f32),
        "norm_g": 1.0 + 0.1 * nrm(ks[4], (D_MODEL,), f32),
        "w_in": nrm(ks[5], (D_MODEL, IN_COLS), f32) * D_MODEL ** -0.5,
        "ln_v_g": 1.0 + 0.1 * nrm(ks[6], (A_WIDTH,), f32),
        "ln_v_b": 0.1 * nrm(ks[7], (A_WIDTH,), f32),
        "w_spatial": 0.5 * nrm(ks[8], (A_GROUPS, CHUNK, CHUNK), f32) * CHUNK ** -0.5,
        "b_spatial": 1.0 + 0.1 * nrm(ks[9], (A_GROUPS, CHUNK), f32),
        "gn_a": 1.0 + 0.1 * nrm(ks[10], (A_WIDTH,), f32),
        "gn_b": 1.0 + 0.1 * nrm(ks[11], (B_WIDTH,), f32),
        "w_out": nrm(ks[12], (D_MIX, D_MODEL), f32) * D_MIX ** -0.5,
        "final_g": 1.0 + 0.1 * nrm(ks[13], (D_MODEL,), f32),
    }


def reference(x_prompt, x_sample, cache_k_win, cache_v_win, norm_g, w_in, ln_v_g, ln_v_b,
              w_spatial, b_spatial, gn_a, gn_b, w_out, final_g):
    Bp, S, _ = x_prompt.shape
    pos_p = jnp.arange(S)
    u, va, ga, q, k, vb, gb = _project(x_prompt, pos_p, norm_g, w_in, ln_v_g, ln_v_b)
    nC = S // CHUNK
    a_p = _spatial_gate(u.reshape(Bp, nC, CHUNK, A_GROUPS, A_GROUP_DIM),
                        va.reshape(Bp, nC, CHUNK, A_GROUPS, A_GROUP_DIM),
                        w_spatial, b_spatial).reshape(Bp, S, A_WIDTH)
    unit = BAND * MAX_DIL
    s_pad = -(-S // unit) * unit
    padw = ((0, 0), (0, s_pad - S), (0, 0), (0, 0))
    qp, kp, vp = jnp.pad(q, padw), jnp.pad(k, padw), jnp.pad(vb, padw)
    outs, lses = [], []
    for win, dil in DILATED_CONFIGS:
        o, l = _dilated_branch_prompt(qp, kp, vp, dil, win // dil)
        outs.append(o[:, :S])
        lses.append(l[:, :S])
    b_p = _combine(outs, lses)
    y_prompt = _finish(x_prompt, a_p, ga, b_p, gb, gn_a, gn_b, w_out, final_g)
    w_keep = min(WIN_MAX, S)
    k_win_prompt = k[:, S - w_keep:]
    v_win_prompt = vb[:, S - w_keep:]

    Bd, T, _ = x_sample.shape
    pos_s = PAST_LEN + jnp.arange(T)
    us, vas, gas, qs, kns, vns, gbs = _project(x_sample, pos_s, norm_g, w_in, ln_v_g, ln_v_b)
    a_s = _spatial_gate(us.reshape(Bd, 1, T, A_GROUPS, A_GROUP_DIM),
                        vas.reshape(Bd, 1, T, A_GROUPS, A_GROUP_DIM),
                        w_spatial, b_spatial).reshape(Bd, T, A_WIDTH)
    k_all = jnp.concatenate([cache_k_win.astype(kns.dtype), kns], axis=1)
    v_all = jnp.concatenate([cache_v_win.astype(vns.dtype), vns], axis=1)
    outs_s, lses_s = [], []
    for win, dil in DILATED_CONFIGS:
        o, l = _dilated_branch_sample(qs, k_all, v_all, dil, win // dil)
        outs_s.append(o)
        lses_s.append(l)
    b_s = _combine(outs_s, lses_s)
    y_sample = _finish(x_sample, a_s, gas, b_s, gbs, gn_a, gn_b, w_out, final_g)

    return (y_prompt, y_sample, k_win_prompt, v_win_prompt, kns, vns, vas)
```

```python
import functools

import numpy as np
import jax
import jax.numpy as jnp
from jax import lax
from jax.experimental import pallas as pl
from jax.experimental.pallas import tpu as pltpu

D_MODEL = 1024
A_WIDTH = 512
B_WIDTH = 512
A_GROUPS = 4
CHUNK = 128
HEADS = 8
HEAD_DIM = 64
ROT_DIM = 16
ROPE_THETA = 500000.0
CONFIGS = ((128, 1), (512, 4), (2048, 16))
STRIDES_PER_WINDOW = 128
BAND = 128
WIN_MAX = 2048
EPS = 1e-6
IN_COLS = 3 * A_WIDTH + 4 * B_WIDTH

LANES = 128
HEADS_PER_SLAB = LANES // HEAD_DIM
N_SLABS = B_WIDTH // LANES
VMEM_LIMIT = 48 * 1024 * 1024

F32 = jnp.float32
BF16 = jnp.bfloat16


def _rms(x, g):
    return x * lax.rsqrt(jnp.mean(x * x, axis=-1, keepdims=True) + EPS) * g


def _silu(x):
    return x / (1.0 + jnp.exp(-x))


def _proj_kernel(x_ref, ng_ref, w_ref, lng_ref, lnb_ref, c_ref, sn_ref, sp_ref,
                 u_ref, va_ref, sga_ref, sgb_ref, q_ref, k_ref, v_ref):
    h = _rms(x_ref[...], ng_ref[...]).astype(BF16)

    def col(g):
        return jnp.dot(h, w_ref[:, g * A_WIDTH:(g + 1) * A_WIDTH], preferred_element_type=F32)

    def rope(z, scale):
        c, sn, sp = c_ref[...], sn_ref[...], sp_ref[...]
        outs = []
        for s in range(N_SLABS):
            zs = z[:, s * LANES:(s + 1) * LANES]
            up = pltpu.roll(zs, LANES - ROT_DIM // 2, 1)
            dn = pltpu.roll(zs, ROT_DIM // 2, 1)
            r = zs * c + up * sn + dn * sp
            outs.append(r * scale if scale != 1.0 else r)
        return jnp.concatenate(outs, axis=1)

    u_ref[...] = jax.nn.gelu(col(0)).astype(u_ref.dtype)
    va = jax.nn.gelu(col(1))
    mu = jnp.mean(va, axis=-1, keepdims=True)
    vc = va - mu
    va = vc * lax.rsqrt(jnp.mean(vc * vc, axis=-1, keepdims=True) + EPS) * lng_ref[...] + lnb_ref[...]
    va_ref[...] = va.astype(va_ref.dtype)
    sga_ref[...] = _silu(col(2)).astype(sga_ref.dtype)
    q_ref[...] = rope(col(3), HEAD_DIM ** -0.5)
    k_ref[...] = rope(col(4), 1.0)
    v_ref[...] = col(5)
    sgb_ref[...] = _silu(col(6)).astype(sgb_ref.dtype)


def _proj(x, norm_g, w_bf, ln_g, ln_b, tabs, tab_blocks, va_dtype, tm):
    n = x.shape[0]
    steps = n // tm
    row = lambda i: (i, 0)
    fixed = lambda i: (0, 0)
    tab = lambda i: (i % tab_blocks, 0)
    out_shape = [
        jax.ShapeDtypeStruct((n, A_WIDTH), BF16),
        jax.ShapeDtypeStruct((n, A_WIDTH), va_dtype),
        jax.ShapeDtypeStruct((n, A_WIDTH), BF16),
        jax.ShapeDtypeStruct((n, B_WIDTH), BF16),
        jax.ShapeDtypeStruct((n, B_WIDTH), F32),
        jax.ShapeDtypeStruct((n, B_WIDTH), F32),
        jax.ShapeDtypeStruct((n, B_WIDTH), F32),
    ]
    return pl.pallas_call(
        _proj_kernel,
        grid=(steps,),
        in_specs=[
            pl.BlockSpec((tm, D_MODEL), row),
            pl.BlockSpec((1, D_MODEL), fixed),
            pl.BlockSpec((D_MODEL, IN_COLS), fixed),
            pl.BlockSpec((1, A_WIDTH), fixed),
            pl.BlockSpec((1, A_WIDTH), fixed),
            pl.BlockSpec((tm, LANES), tab),
            pl.BlockSpec((tm, LANES), tab),
            pl.BlockSpec((tm, LANES), tab),
        ],
        out_specs=[pl.BlockSpec((tm, A_WIDTH), row)] * 7,
        out_shape=out_shape,
        compiler_params=pltpu.CompilerParams(
            dimension_semantics=("arbitrary",), vmem_limit_bytes=VMEM_LIMIT),
        name="proj",
    )(x, norm_g, w_bf, ln_g, ln_b, *tabs)


def _rope_tables(pos):
    inv = ROPE_THETA ** (-jnp.arange(0, ROT_DIM, 2, dtype=F32) / ROT_DIM)
    ang = pos.astype(F32)[:, None] * inv[None, :]
    cos, sin = jnp.cos(ang), jnp.sin(ang)
    half = ROT_DIM // 2
    ones = jnp.ones((pos.shape[0], HEAD_DIM - ROT_DIM), F32)
    zeros = jnp.zeros((pos.shape[0], HEAD_DIM - half), F32)
    c = jnp.concatenate([cos, cos, ones], axis=1)
    sn = jnp.concatenate([-sin, zeros], axis=1)
    sp = jnp.concatenate([zeros[:, :half], sin, zeros[:, :HEAD_DIM - ROT_DIM]], axis=1)
    rep = lambda t: jnp.tile(t, (1, HEADS_PER_SLAB))
    return rep(c), rep(sn), rep(sp)


ATT_TILE = WIN_MAX


def _attn_kernel(q_ref, k_ref, v_ref, o_ref, osc, lsc):
    seq = q_ref.shape[1]
    lane = lax.broadcasted_iota(jnp.int32, (BAND, LANES), 1)
    head0 = lane < HEAD_DIM
    ii = lax.broadcasted_iota(jnp.int32, (BAND, 2 * BAND), 0)
    jj = lax.broadcasted_iota(jnp.int32, (BAND, 2 * BAND), 1)
    near_enough = jj <= ii + BAND

    def unit(c, dil, cls, m0, t):
        q = q_ref[0, pl.ds(cls + dil * m0, BAND, stride=dil), :]
        mp = jnp.maximum(m0 - BAND, 0)

        def two_blocks(ref):
            prev = ref[0, pl.ds(cls + dil * mp, BAND, stride=dil), :]
            cur = ref[0, pl.ds(cls + dil * m0, BAND, stride=dil), :]
            return jnp.concatenate([prev, cur], axis=0).astype(BF16)

        kk = two_blocks(k_ref)
        vv = two_blocks(v_ref)
        valid = near_enough & (jj >= jnp.maximum(ii, jnp.where(m0 > 0, 0, BAND)))
        outs, lses = [], []
        for hmask in (head0, jnp.logical_not(head0)):
            qh = jnp.where(hmask, q, 0.0).astype(BF16)
            s = lax.dot_general(qh, kk, (((1,), (1,)), ((), ())), preferred_element_type=F32)
            s = jnp.where(valid, s, -jnp.inf)
            m = jnp.max(s, axis=1, keepdims=True)
            p = jnp.exp(s - m)
            l = jnp.sum(p, axis=1, keepdims=True)
            pv = jnp.dot(p.astype(BF16), vv, preferred_element_type=F32)
            outs.append(pv / l)
            lses.append(m + jnp.log(l))
        row0 = cls + dil * m0 - t * ATT_TILE
        osc[c, pl.ds(row0, BAND, stride=dil), :] = jnp.where(head0, outs[0], outs[1])
        lsc[c, pl.ds(row0, BAND, stride=dil), :] = jnp.where(
            head0, jnp.broadcast_to(lses[0], (BAND, LANES)), jnp.broadcast_to(lses[1], (BAND, LANES)))

    units = ATT_TILE // BAND

    def tile_body(t, carry):
        for c, (_, dil) in enumerate(CONFIGS):
            blocks = units // dil

            def body(u, carry2, c=c, dil=dil, blocks=blocks):
                cls = u // blocks
                nb = u - cls * blocks
                unit(c, dil, cls, t * (ATT_TILE // dil) + nb * BAND, t)
                return carry2

            lax.fori_loop(0, units, body, 0)

        rows = 256
        for r in range(ATT_TILE // rows):
            sl = pl.ds(r * rows, rows)
            ls = [lsc[c, sl, :] for c in range(len(CONFIGS))]
            mx = jnp.maximum(jnp.maximum(ls[0], ls[1]), ls[2])
            ws = [jnp.exp(l - mx) for l in ls]
            num = ws[0] * osc[0, sl, :] + ws[1] * osc[1, sl, :] + ws[2] * osc[2, sl, :]
            o_ref[0, pl.ds(pl.multiple_of(t * ATT_TILE + r * rows, rows), rows), :] = (
                num / (ws[0] + ws[1] + ws[2])).astype(o_ref.dtype)
        return carry

    lax.fori_loop(0, seq // ATT_TILE, tile_body, 0)


def _attn(q, k, v):
    bn, seq, _ = q.shape
    spec = pl.BlockSpec((1, seq, LANES), lambda b, s: (b, 0, s))
    return pl.pallas_call(
        _attn_kernel,
        grid=(bn, N_SLABS),
        in_specs=[spec, spec, spec],
        out_specs=spec,
        out_shape=jax.ShapeDtypeStruct((bn, seq, B_WIDTH), BF16),
        scratch_shapes=[pltpu.VMEM((len(CONFIGS), ATT_TILE, LANES), F32),
                        pltpu.VMEM((len(CONFIGS), ATT_TILE, LANES), F32)],
        compiler_params=pltpu.CompilerParams(
            dimension_semantics=("arbitrary", "arbitrary"), vmem_limit_bytes=VMEM_LIMIT),
        name="attn",
    )(q, k, v)


NEW_PAD = 128


def _sample_counts(w_buf, t_new):
    tq = np.arange(t_new)[:, None]

    def count(idx, exists):
        dist = w_buf + tq - idx[None, :]
        cnt = np.zeros(dist.shape, np.float32)
        for win, dil in CONFIGS:
            cnt += ((dist >= 0) & (dist <= win) & (dist % dil == 0) & exists[None, :])
        return np.tile(cnt, (HEADS, 1))

    old = np.arange(w_buf)
    new = np.arange(w_buf, w_buf + NEW_PAD)
    return count(old, old >= 0), count(new, new < w_buf + t_new)


def _sattn_kernel(q_ref, kn_ref, vn_ref, kt_ref, vt_ref, cnt_old_ref, cnt_new_ref, o_ref):
    t_new = q_ref.shape[0]
    rows = HEADS * t_new
    qrep = jnp.concatenate([q_ref[...]] * HEADS, axis=0)
    rh = lax.broadcasted_iota(jnp.int32, (rows, B_WIDTH), 0) // t_new
    lh = lax.broadcasted_iota(jnp.int32, (rows, B_WIDTH), 1) // HEAD_DIM
    own = rh == lh
    qbd = jnp.where(own, qrep, 0.0).astype(BF16)
    pad = jnp.zeros((NEW_PAD - t_new, B_WIDTH), F32)
    kn = jnp.concatenate([kn_ref[...], pad], axis=0).astype(BF16)
    vn = jnp.concatenate([vn_ref[...], pad], axis=0).astype(BF16)
    cnt_old, cnt_new = cnt_old_ref[...], cnt_new_ref[...]
    s_old = jnp.dot(qbd, kt_ref[0].astype(BF16), preferred_element_type=F32)
    s_new = lax.dot_general(qbd, kn, (((1,), (1,)), ((), ())), preferred_element_type=F32)
    s_old = jnp.where(cnt_old > 0.0, s_old, -jnp.inf)
    s_new = jnp.where(cnt_new > 0.0, s_new, -jnp.inf)
    m = jnp.maximum(jnp.max(s_old, axis=1, keepdims=True), jnp.max(s_new, axis=1, keepdims=True))
    p_old = jnp.exp(s_old - m) * cnt_old
    p_new = jnp.exp(s_new - m) * cnt_new
    l = jnp.sum(p_old, axis=1, keepdims=True) + jnp.sum(p_new, axis=1, keepdims=True)
    o = lax.dot_general(p_old.astype(BF16), vt_ref[0].astype(BF16), (((1,), (1,)), ((), ())),
                        preferred_element_type=F32)
    o = (o + jnp.dot(p_new.astype(BF16), vn, preferred_element_type=F32)) / l
    o = jnp.where(own, o, 0.0)
    acc = o[0:t_new]
    for h in range(1, HEADS):
        acc = acc + o[h * t_new:(h + 1) * t_new]
    o_ref[...] = acc.astype(o_ref.dtype)


def _sattn(q, k_new, v_new, cache_kt, cache_vt, t_new):
    n = q.shape[0]
    bd, _, w_buf = cache_kt.shape
    cnt_old, cnt_new = (jnp.asarray(c) for c in _sample_counts(w_buf, t_new))
    new_spec = pl.BlockSpec((t_new, B_WIDTH), lambda b: (b, 0))
    buf_spec = pl.BlockSpec((1, B_WIDTH, w_buf), lambda b: (b, 0, 0))
    return pl.pallas_call(
        _sattn_kernel,
        grid=(bd,),
        in_specs=[new_spec, new_spec, new_spec, buf_spec, buf_spec,
                  pl.BlockSpec(cnt_old.shape, lambda b: (0, 0)),
                  pl.BlockSpec(cnt_new.shape, lambda b: (0, 0))],
        out_specs=new_spec,
        out_shape=jax.ShapeDtypeStruct((n, B_WIDTH), BF16),
        compiler_params=pltpu.CompilerParams(
            dimension_semantics=("arbitrary",), vmem_limit_bytes=VMEM_LIMIT),
        name="sattn",
    )(q, k_new, v_new, cache_kt, cache_vt, cnt_old, cnt_new)


def _transpose_kernel(k_ref, v_ref, kt_ref, vt_ref):
    kt_ref[0] = k_ref[...].T
    vt_ref[0] = v_ref[...].T


def _transpose_blocks(k, v, grid, src_block, src_map, out_dims, dst_map):
    src = pl.BlockSpec(src_block, src_map)
    dst = pl.BlockSpec((1, src_block[1], src_block[0]), dst_map)
    shape = jax.ShapeDtypeStruct(out_dims, F32)
    return pl.pallas_call(
        _transpose_kernel, grid=grid, in_specs=[src, src], out_specs=[dst, dst],
        out_shape=[shape, shape],
        compiler_params=pltpu.CompilerParams(dimension_semantics=("arbitrary",) * len(grid)),
        name="transpose",
    )(k, v)


def _finish_kernel(x_ref, u_ref, va_ref, sga_ref, b_ref, sgb_ref, wsp_ref, bsp_ref,
                   gna_ref, gnb_ref, wout_ref, fg_ref, y_ref):
    tm = x_ref.shape[0]
    nc = tm // CHUNK
    ri = lax.broadcasted_iota(jnp.int32, (CHUNK, CHUNK), 0)
    ci = lax.broadcasted_iota(jnp.int32, (CHUNK, CHUNK), 1)
    causal = ri >= ci
    va = va_ref[...].astype(BF16)
    u = u_ref[...].astype(F32)
    a_cols = []
    for g in range(A_GROUPS):
        wm = jnp.where(causal, wsp_ref[g], 0.0).astype(BF16)
        lanes = slice(g * CHUNK, (g + 1) * CHUNK)
        rhs = jnp.concatenate([va[c * CHUNK:(c + 1) * CHUNK, lanes] for c in range(nc)], axis=1)
        s = jnp.dot(wm, rhs, preferred_element_type=F32) + bsp_ref[g]
        s = jnp.concatenate([s[:, c * CHUNK:(c + 1) * CHUNK] for c in range(nc)], axis=0)
        a_cols.append(u[:, lanes] * s)
    a = jnp.concatenate(a_cols, axis=1)
    ha = _rms(a * sga_ref[...].astype(F32), gna_ref[...])
    hb = _rms(b_ref[...].astype(F32) * sgb_ref[...].astype(F32), gnb_ref[...])
    hcat = jnp.concatenate([ha, hb], axis=1).astype(BF16)
    y = x_ref[...] + jnp.dot(hcat, wout_ref[...], preferred_element_type=F32)
    y_ref[...] = _rms(y, fg_ref[...])


def _finish(x, u, va, sga, b, sgb, wsp, bsp, gn_a, gn_b, wout_bf, final_g, tm):
    n = x.shape[0]
    row = lambda i: (i, 0)
    fixed2 = lambda i: (0, 0)
    fixed3 = lambda i: (0, 0, 0)
    half = pl.BlockSpec((tm, A_WIDTH), row)
    return pl.pallas_call(
        _finish_kernel,
        grid=(n // tm,),
        in_specs=[
            pl.BlockSpec((tm, D_MODEL), row), half, half, half, half, half,
            pl.BlockSpec((A_GROUPS, CHUNK, CHUNK), fixed3),
            pl.BlockSpec((A_GROUPS, CHUNK, 1), fixed3),
            pl.BlockSpec((1, A_WIDTH), fixed2),
            pl.BlockSpec((1, B_WIDTH), fixed2),
            pl.BlockSpec((D_MODEL, D_MODEL), fixed2),
            pl.BlockSpec((1, D_MODEL), fixed2),
        ],
        out_specs=pl.BlockSpec((tm, D_MODEL), row),
        out_shape=jax.ShapeDtypeStruct((n, D_MODEL), F32),
        compiler_params=pltpu.CompilerParams(
            dimension_semantics=("arbitrary",), vmem_limit_bytes=VMEM_LIMIT),
        name="finish",
    )(x, u, va, sga, b, sgb, wsp, bsp, gn_a, gn_b, wout_bf, final_g)


def kernel(x_prompt, x_sample, cache_k_win, cache_v_win, norm_g, w_in, ln_v_g, ln_v_b,
           w_spatial, b_spatial, gn_a, gn_b, w_out, final_g):
    bp, seq, _ = x_prompt.shape
    bd, t_new, _ = x_sample.shape
    past_len = seq
    w_in_bf = w_in.astype(BF16)
    w_out_bf = w_out.astype(BF16)
    ng = norm_g.reshape(1, D_MODEL)
    lg, lb = ln_v_g.reshape(1, A_WIDTH), ln_v_b.reshape(1, A_WIDTH)
    ga, gb = gn_a.reshape(1, A_WIDTH), gn_b.reshape(1, B_WIDTH)
    fg = final_g.reshape(1, D_MODEL)
    tm = 512

    xp = x_prompt.reshape(bp * seq, D_MODEL)
    tabs_p = _rope_tables(jnp.arange(seq))
    u, va, sga, sgb, q, k, v = _proj(xp, ng, w_in_bf, lg, lb, tabs_p, seq // tm, BF16, tm)
    b_attn = _attn(q.reshape(bp, seq, B_WIDTH), k.reshape(bp, seq, B_WIDTH),
                   v.reshape(bp, seq, B_WIDTH))
    bsp = b_spatial.reshape(A_GROUPS, CHUNK, 1)
    y_prompt = _finish(xp, u, va, sga, b_attn.reshape(bp * seq, B_WIDTH), sgb, w_spatial, bsp,
                       ga, gb, w_out_bf, fg, tm).reshape(bp, seq, D_MODEL)
    w_keep = min(WIN_MAX, seq)
    per_seq, first = seq // tm, (seq - w_keep) // tm
    k_win_t, v_win_t = _transpose_blocks(
        k, v, (bp, w_keep // tm), (tm, B_WIDTH), lambda b, j: (b * per_seq + first + j, 0),
        (bp, B_WIDTH, w_keep), lambda b, j: (b, 0, j))
    k_win = k_win_t.reshape(bp, HEADS, HEAD_DIM, w_keep).transpose(0, 3, 1, 2)
    v_win = v_win_t.reshape(bp, HEADS, HEAD_DIM, w_keep).transpose(0, 3, 1, 2)

    xs = x_sample.reshape(bd * t_new, D_MODEL)
    pos_s = past_len + jnp.tile(jnp.arange(t_new), bd)
    tabs_s = _rope_tables(pos_s)
    n_s = bd * t_new
    us, vas, sgas, sgbs, qs, ks, vs = _proj(xs, ng, w_in_bf, lg, lb, tabs_s, n_s // tm, F32, tm)
    w_buf = cache_k_win.shape[1]
    cache_kt = cache_k_win.transpose(0, 2, 3, 1).reshape(bd, B_WIDTH, w_buf)
    cache_vt = cache_v_win.transpose(0, 2, 3, 1).reshape(bd, B_WIDTH, w_buf)
    b_s = _sattn(qs, ks, vs, cache_kt, cache_vt, t_new)
    k_new_t, v_new_t = _transpose_blocks(
        ks.reshape(bd, t_new * B_WIDTH), vs.reshape(bd, t_new * B_WIDTH), (t_new,),
        (bd, B_WIDTH), lambda t: (0, t), (t_new, B_WIDTH, bd), lambda t: (t, 0, 0))
    k_new = k_new_t.reshape(t_new, HEADS, HEAD_DIM, bd).transpose(3, 0, 1, 2)
    v_new = v_new_t.reshape(t_new, HEADS, HEAD_DIM, bd).transpose(3, 0, 1, 2)
    seqs_per_chunk = CHUNK // t_new
    eye = jnp.eye(seqs_per_chunk, dtype=w_spatial.dtype)
    wsp_s = jax.vmap(lambda w: jnp.kron(eye, w))(w_spatial[:, :t_new, :t_new])
    bsp_s = jnp.tile(b_spatial[:, :t_new], (1, seqs_per_chunk)).reshape(A_GROUPS, CHUNK, 1)
    y_sample = _finish(xs, us, vas, sgas, b_s, sgbs, wsp_s, bsp_s, ga, gb, w_out_bf, fg,
                       tm).reshape(bd, t_new, D_MODEL)

    return (y_prompt, y_sample, k_win, v_win, k_new, v_new, vas.reshape(bd, t_new, A_WIDTH))
```

```python
import functools

import numpy as np
import jax
import jax.numpy as jnp
from jax import lax
from jax.experimental import pallas as pl
from jax.experimental.pallas import tpu as pltpu

D_MODEL = 1024
A_WIDTH = 512
B_WIDTH = 512
A_GROUPS = 4
CHUNK = 128
HEADS = 8
HEAD_DIM = 64
ROT_DIM = 16
ROPE_THETA = 500000.0
CONFIGS = ((128, 1), (512, 4), (2048, 16))
STRIDES_PER_WINDOW = 128
BAND = 128
WIN_MAX = 2048
EPS = 1e-6
IN_COLS = 3 * A_WIDTH + 4 * B_WIDTH

LANES = 128
HEADS_PER_SLAB = LANES // HEAD_DIM
N_SLABS = B_WIDTH // LANES
VMEM_LIMIT = 48 * 1024 * 1024

F32 = jnp.float32
BF16 = jnp.bfloat16


def _rms(x, g):
    return x * lax.rsqrt(jnp.mean(x * x, axis=-1, keepdims=True) + EPS) * g


def _silu(x):
    return x / (1.0 + jnp.exp(-x))


def _proj_kernel(x_ref, ng_ref, w_ref, lng_ref, lnb_ref, c_ref, sn_ref, sp_ref,
                 u_ref, va_ref, sga_ref, sgb_ref, q_ref, k_ref, v_ref):
    h = _rms(x_ref[...], ng_ref[...]).astype(BF16)

    def col(g):
        return jnp.dot(h, w_ref[:, g * A_WIDTH:(g + 1) * A_WIDTH], preferred_element_type=F32)

    def rope(z, scale):
        c, sn, sp = c_ref[...], sn_ref[...], sp_ref[...]
        outs = []
        for s in range(N_SLABS):
            zs = z[:, s * LANES:(s + 1) * LANES]
            up = pltpu.roll(zs, LANES - ROT_DIM // 2, 1)
            dn = pltpu.roll(zs, ROT_DIM // 2, 1)
            r = zs * c + up * sn + dn * sp
            outs.append(r * scale if scale != 1.0 else r)
        return jnp.concatenate(outs, axis=1)

    u_ref[...] = jax.nn.gelu(col(0)).astype(u_ref.dtype)
    va = jax.nn.gelu(col(1))
    mu = jnp.mean(va, axis=-1, keepdims=True)
    vc = va - mu
    va = vc * lax.rsqrt(jnp.mean(vc * vc, axis=-1, keepdims=True) + EPS) * lng_ref[...] + lnb_ref[...]
    va_ref[...] = va.astype(va_ref.dtype)
    sga_ref[...] = _silu(col(2)).astype(sga_ref.dtype)
    q_ref[...] = rope(col(3), HEAD_DIM ** -0.5)
    k_ref[...] = rope(col(4), 1.0)
    v_ref[...] = col(5)
    sgb_ref[...] = _silu(col(6)).astype(sgb_ref.dtype)


def _proj(x, norm_g, w_bf, ln_g, ln_b, tabs, tab_blocks, va_dtype, tm):
    n = x.shape[0]
    steps = n // tm
    row = lambda i: (i, 0)
    fixed = lambda i: (0, 0)
    tab = lambda i: (i % tab_blocks, 0)
    out_shape = [
        jax.ShapeDtypeStruct((n, A_WIDTH), BF16),
        jax.ShapeDtypeStruct((n, A_WIDTH), va_dtype),
        jax.ShapeDtypeStruct((n, A_WIDTH), BF16),
        jax.ShapeDtypeStruct((n, B_WIDTH), BF16),
        jax.ShapeDtypeStruct((n, B_WIDTH), F32),
        jax.ShapeDtypeStruct((n, B_WIDTH), F32),
        jax.ShapeDtypeStruct((n, B_WIDTH), F32),
    ]
    return pl.pallas_call(
        _proj_kernel,
        grid=(steps,),
        in_specs=[
            pl.BlockSpec((tm, D_MODEL), row),
            pl.BlockSpec((1, D_MODEL), fixed),
            pl.BlockSpec((D_MODEL, IN_COLS), fixed),
            pl.BlockSpec((1, A_WIDTH), fixed),
            pl.BlockSpec((1, A_WIDTH), fixed),
            pl.BlockSpec((tm, LANES), tab),
            pl.BlockSpec((tm, LANES), tab),
            pl.BlockSpec((tm, LANES), tab),
        ],
        out_specs=[pl.BlockSpec((tm, A_WIDTH), row)] * 7,
        out_shape=out_shape,
        compiler_params=pltpu.CompilerParams(
            dimension_semantics=("arbitrary",), vmem_limit_bytes=VMEM_LIMIT),
        name="proj",
    )(x, norm_g, w_bf, ln_g, ln_b, *tabs)


def _rope_tables(pos):
    inv = ROPE_THETA ** (-jnp.arange(0, ROT_DIM, 2, dtype=F32) / ROT_DIM)
    ang = pos.astype(F32)[:, None] * inv[None, :]
    cos, sin = jnp.cos(ang), jnp.sin(ang)
    half = ROT_DIM // 2
    ones = jnp.ones((pos.shape[0], HEAD_DIM - ROT_DIM), F32)
    zeros = jnp.zeros((pos.shape[0], HEAD_DIM - half), F32)
    c = jnp.concatenate([cos, cos, ones], axis=1)
    sn = jnp.concatenate([-sin, zeros], axis=1)
    sp = jnp.concatenate([zeros[:, :half], sin, zeros[:, :HEAD_DIM - ROT_DIM]], axis=1)
    rep = lambda t: jnp.tile(t, (1, HEADS_PER_SLAB))
    return rep(c), rep(sn), rep(sp)


ATT_TILE = WIN_MAX
UNIT_UNROLL = 8


def _attn_kernel(q_ref, k_ref, v_ref, o_ref, osc, lsc):
    seq = q_ref.shape[1]
    lane = lax.broadcasted_iota(jnp.int32, (BAND, LANES), 1)
    head0 = lane < HEAD_DIM
    ii = lax.broadcasted_iota(jnp.int32, (BAND, 2 * BAND), 0)
    jj = lax.broadcasted_iota(jnp.int32, (BAND, 2 * BAND), 1)
    near_enough = jj <= ii + BAND

    def unit(c, dil, cls, m0, t):
        q = q_ref[0, pl.ds(cls + dil * m0, BAND, stride=dil), :]
        mp = jnp.maximum(m0 - BAND, 0)

        def two_blocks(ref):
            prev = ref[0, pl.ds(cls + dil * mp, BAND, stride=dil), :]
            cur = ref[0, pl.ds(cls + dil * m0, BAND, stride=dil), :]
            return jnp.concatenate([prev, cur], axis=0).astype(BF16)

        kk = two_blocks(k_ref)
        vv = two_blocks(v_ref)
        valid = near_enough & (jj >= jnp.maximum(ii, jnp.where(m0 > 0, 0, BAND)))
        outs, lses = [], []
        for hmask in (head0, jnp.logical_not(head0)):
            qh = jnp.where(hmask, q, 0.0).astype(BF16)
            s = lax.dot_general(qh, kk, (((1,), (1,)), ((), ())), preferred_element_type=F32)
            s = jnp.where(valid, s, -jnp.inf)
            m = jnp.max(s, axis=1, keepdims=True)
            p = jnp.exp(s - m)
            l = jnp.sum(p, axis=1, keepdims=True)
            pv = jnp.dot(p.astype(BF16), vv, preferred_element_type=F32)
            outs.append(pv / l)
            lses.append(m + jnp.log(l))
        row0 = cls + dil * m0 - t * ATT_TILE
        osc[c, pl.ds(row0, BAND, stride=dil), :] = jnp.where(head0, outs[0], outs[1])
        lsc[c, pl.ds(row0, BAND, stride=dil), :] = jnp.where(
            head0, jnp.broadcast_to(lses[0], (BAND, LANES)), jnp.broadcast_to(lses[1], (BAND, LANES)))

    units = ATT_TILE // BAND

    def tile_body(t, carry):
        for c, (_, dil) in enumerate(CONFIGS):
            blocks = units // dil

            def body(u, carry2, c=c, dil=dil, blocks=blocks):
                cls = u // blocks
                nb = u - cls * blocks
                unit(c, dil, cls, t * (ATT_TILE // dil) + nb * BAND, t)
                return carry2

            lax.fori_loop(0, units, body, 0, unroll=UNIT_UNROLL)

        rows = 256
        for r in range(ATT_TILE // rows):
            sl = pl.ds(r * rows, rows)
            ls = [lsc[c, sl, :] for c in range(len(CONFIGS))]
            mx = jnp.maximum(jnp.maximum(ls[0], ls[1]), ls[2])
            ws = [jnp.exp(l - mx) for l in ls]
            num = ws[0] * osc[0, sl, :] + ws[1] * osc[1, sl, :] + ws[2] * osc[2, sl, :]
            o_ref[0, pl.ds(pl.multiple_of(t * ATT_TILE + r * rows, rows), rows), :] = (
                num / (ws[0] + ws[1] + ws[2])).astype(o_ref.dtype)
        return carry

    lax.fori_loop(0, seq // ATT_TILE, tile_body, 0)


def _attn(q, k, v):
    bn, seq, _ = q.shape
    spec = pl.BlockSpec((1, seq, LANES), lambda b, s: (b, 0, s))
    return pl.pallas_call(
        _attn_kernel,
        grid=(bn, N_SLABS),
        in_specs=[spec, spec, spec],
        out_specs=spec,
        out_shape=jax.ShapeDtypeStruct((bn, seq, B_WIDTH), BF16),
        scratch_shapes=[pltpu.VMEM((len(CONFIGS), ATT_TILE, LANES), F32),
                        pltpu.VMEM((len(CONFIGS), ATT_TILE, LANES), F32)],
        compiler_params=pltpu.CompilerParams(
            dimension_semantics=("arbitrary", "arbitrary"), vmem_limit_bytes=VMEM_LIMIT),
        name="attn",
    )(q, k, v)


NEW_PAD = 128


def _sample_counts(w_buf, t_new):
    tq = np.arange(t_new)[:, None]

    def count(idx, exists):
        dist = w_buf + tq - idx[None, :]
        cnt = np.zeros(dist.shape, np.float32)
        for win, dil in CONFIGS:
            cnt += ((dist >= 0) & (dist <= win) & (dist % dil == 0) & exists[None, :])
        return np.tile(cnt, (HEADS, 1))

    old = np.arange(w_buf)
    new = np.arange(w_buf, w_buf + NEW_PAD)
    return count(old, old >= 0), count(new, new < w_buf + t_new)


def _sattn_kernel(q_ref, kn_ref, vn_ref, kt_ref, vt_ref, cnt_old_ref, cnt_new_ref, o_ref):
    t_new = q_ref.shape[0]
    rows = HEADS * t_new
    qrep = jnp.concatenate([q_ref[...]] * HEADS, axis=0)
    rh = lax.broadcasted_iota(jnp.int32, (rows, B_WIDTH), 0) // t_new
    lh = lax.broadcasted_iota(jnp.int32, (rows, B_WIDTH), 1) // HEAD_DIM
    own = rh == lh
    qbd = jnp.where(own, qrep, 0.0).astype(BF16)
    pad = jnp.zeros((NEW_PAD - t_new, B_WIDTH), F32)
    kn = jnp.concatenate([kn_ref[...], pad], axis=0).astype(BF16)
    vn = jnp.concatenate([vn_ref[...], pad], axis=0).astype(BF16)
    cnt_old, cnt_new = cnt_old_ref[...], cnt_new_ref[...]
    s_old = jnp.dot(qbd, kt_ref[0].astype(BF16), preferred_element_type=F32)
    s_new = lax.dot_general(qbd, kn, (((1,), (1,)), ((), ())), preferred_element_type=F32)
    s_old = jnp.where(cnt_old > 0.0, s_old, -jnp.inf)
    s_new = jnp.where(cnt_new > 0.0, s_new, -jnp.inf)
    m = jnp.maximum(jnp.max(s_old, axis=1, keepdims=True), jnp.max(s_new, axis=1, keepdims=True))
    p_old = jnp.exp(s_old - m) * cnt_old
    p_new = jnp.exp(s_new - m) * cnt_new
    l = jnp.sum(p_old, axis=1, keepdims=True) + jnp.sum(p_new, axis=1, keepdims=True)
    o = lax.dot_general(p_old.astype(BF16), vt_ref[0].astype(BF16), (((1,), (1,)), ((), ())),
                        preferred_element_type=F32)
    o = (o + jnp.dot(p_new.astype(BF16), vn, preferred_element_type=F32)) / l
    o = jnp.where(own, o, 0.0)
    acc = o[0:t_new]
    for h in range(1, HEADS):
        acc = acc + o[h * t_new:(h + 1) * t_new]
    o_ref[...] = acc.astype(o_ref.dtype)


def _sattn(q, k_new, v_new, cache_kt, cache_vt, t_new):
    n = q.shape[0]
    bd, _, w_buf = cache_kt.shape
    cnt_old, cnt_new = (jnp.asarray(c) for c in _sample_counts(w_buf, t_new))
    new_spec = pl.BlockSpec((t_new, B_WIDTH), lambda b: (b, 0))
    buf_spec = pl.BlockSpec((1, B_WIDTH, w_buf), lambda b: (b, 0, 0))
    return pl.pallas_call(
        _sattn_kernel,
        grid=(bd,),
        in_specs=[new_spec, new_spec, new_spec, buf_spec, buf_spec,
                  pl.BlockSpec(cnt_old.shape, lambda b: (0, 0)),
                  pl.BlockSpec(cnt_new.shape, lambda b: (0, 0))],
        out_specs=new_spec,
        out_shape=jax.ShapeDtypeStruct((n, B_WIDTH), BF16),
        compiler_params=pltpu.CompilerParams(
            dimension_semantics=("arbitrary",), vmem_limit_bytes=VMEM_LIMIT),
        name="sattn",
    )(q, k_new, v_new, cache_kt, cache_vt, cnt_old, cnt_new)


def _transpose_kernel(k_ref, v_ref, kt_ref, vt_ref):
    kt_ref[0] = k_ref[...].T
    vt_ref[0] = v_ref[...].T


def _transpose_blocks(k, v, grid, src_block, src_map, out_dims, dst_map):
    src = pl.BlockSpec(src_block, src_map)
    dst = pl.BlockSpec((1, src_block[1], src_block[0]), dst_map)
    shape = jax.ShapeDtypeStruct(out_dims, F32)
    return pl.pallas_call(
        _transpose_kernel, grid=grid, in_specs=[src, src], out_specs=[dst, dst],
        out_shape=[shape, shape],
        compiler_params=pltpu.CompilerParams(dimension_semantics=("arbitrary",) * len(grid)),
        name="transpose",
    )(k, v)


def _finish_kernel(x_ref, u_ref, va_ref, sga_ref, b_ref, sgb_ref, wsp_ref, bsp_ref,
                   gna_ref, gnb_ref, wout_ref, fg_ref, y_ref):
    tm = x_ref.shape[0]
    nc = tm // CHUNK
    ri = lax.broadcasted_iota(jnp.int32, (CHUNK, CHUNK), 0)
    ci = lax.broadcasted_iota(jnp.int32, (CHUNK, CHUNK), 1)
    causal = ri >= ci
    va = va_ref[...].astype(BF16)
    u = u_ref[...].astype(F32)
    a_cols = []
    for g in range(A_GROUPS):
        wm = jnp.where(causal, wsp_ref[g], 0.0).astype(BF16)
        lanes = slice(g * CHUNK, (g + 1) * CHUNK)
        rhs = jnp.concatenate([va[c * CHUNK:(c + 1) * CHUNK, lanes] for c in range(nc)], axis=1)
        s = jnp.dot(wm, rhs, preferred_element_type=F32) + bsp_ref[g]
        s = jnp.concatenate([s[:, c * CHUNK:(c + 1) * CHUNK] for c in range(nc)], axis=0)
        a_cols.append(u[:, lanes] * s)
    a = jnp.concatenate(a_cols, axis=1)
    ha = _rms(a * sga_ref[...].astype(F32), gna_ref[...])
    hb = _rms(b_ref[...].astype(F32) * sgb_ref[...].astype(F32), gnb_ref[...])
    hcat = jnp.concatenate([ha, hb], axis=1).astype(BF16)
    y = x_ref[...] + jnp.dot(hcat, wout_ref[...], preferred_element_type=F32)
    y_ref[...] = _rms(y, fg_ref[...])


def _finish(x, u, va, sga, b, sgb, wsp, bsp, gn_a, gn_b, wout_bf, final_g, tm):
    n = x.shape[0]
    row = lambda i: (i, 0)
    fixed2 = lambda i: (0, 0)
    fixed3 = lambda i: (0, 0, 0)
    half = pl.BlockSpec((tm, A_WIDTH), row)
    return pl.pallas_call(
        _finish_kernel,
        grid=(n // tm,),
        in_specs=[
            pl.BlockSpec((tm, D_MODEL), row), half, half, half, half, half,
            pl.BlockSpec((A_GROUPS, CHUNK, CHUNK), fixed3),
            pl.BlockSpec((A_GROUPS, CHUNK, 1), fixed3),
            pl.BlockSpec((1, A_WIDTH), fixed2),
            pl.BlockSpec((1, B_WIDTH), fixed2),
            pl.BlockSpec((D_MODEL, D_MODEL), fixed2),
            pl.BlockSpec((1, D_MODEL), fixed2),
        ],
        out_specs=pl.BlockSpec((tm, D_MODEL), row),
        out_shape=jax.ShapeDtypeStruct((n, D_MODEL), F32),
        compiler_params=pltpu.CompilerParams(
            dimension_semantics=("arbitrary",), vmem_limit_bytes=VMEM_LIMIT),
        name="finish",
    )(x, u, va, sga, b, sgb, wsp, bsp, gn_a, gn_b, wout_bf, final_g)


def kernel(x_prompt, x_sample, cache_k_win, cache_v_win, norm_g, w_in, ln_v_g, ln_v_b,
           w_spatial, b_spatial, gn_a, gn_b, w_out, final_g):
    bp, seq, _ = x_prompt.shape
    bd, t_new, _ = x_sample.shape
    past_len = seq
    w_in_bf = w_in.astype(BF16)
    w_out_bf = w_out.astype(BF16)
    ng = norm_g.reshape(1, D_MODEL)
    lg, lb = ln_v_g.reshape(1, A_WIDTH), ln_v_b.reshape(1, A_WIDTH)
    ga, gb = gn_a.reshape(1, A_WIDTH), gn_b.reshape(1, B_WIDTH)
    fg = final_g.reshape(1, D_MODEL)
    tm = 512

    xp = x_prompt.reshape(bp * seq, D_MODEL)
    tabs_p = _rope_tables(jnp.arange(seq))
    u, va, sga, sgb, q, k, v = _proj(xp, ng, w_in_bf, lg, lb, tabs_p, seq // tm, BF16, tm)
    b_attn = _attn(q.reshape(bp, seq, B_WIDTH), k.reshape(bp, seq, B_WIDTH),
                   v.reshape(bp, seq, B_WIDTH))
    bsp = b_spatial.reshape(A_GROUPS, CHUNK, 1)
    y_prompt = _finish(xp, u, va, sga, b_attn.reshape(bp * seq, B_WIDTH), sgb, w_spatial, bsp,
                       ga, gb, w_out_bf, fg, tm).reshape(bp, seq, D_MODEL)
    w_keep = min(WIN_MAX, seq)
    per_seq, first = seq // tm, (seq - w_keep) // tm
    k_win_t, v_win_t = _transpose_blocks(
        k, v, (bp, w_keep // tm), (tm, B_WIDTH), lambda b, j: (b * per_seq + first + j, 0),
        (bp, B_WIDTH, w_keep), lambda b, j: (b, 0, j))
    k_win = k_win_t.reshape(bp, HEADS, HEAD_DIM, w_keep).transpose(0, 3, 1, 2)
    v_win = v_win_t.reshape(bp, HEADS, HEAD_DIM, w_keep).transpose(0, 3, 1, 2)

    xs = x_sample.reshape(bd * t_new, D_MODEL)
    pos_s = past_len + jnp.tile(jnp.arange(t_new), bd)
    tabs_s = _rope_tables(pos_s)
    n_s = bd * t_new
    us, vas, sgas, sgbs, qs, ks, vs = _proj(xs, ng, w_in_bf, lg, lb, tabs_s, n_s // tm, F32, tm)
    w_buf = cache_k_win.shape[1]
    cache_kt = cache_k_win.transpose(0, 2, 3, 1).reshape(bd, B_WIDTH, w_buf)
    cache_vt = cache_v_win.transpose(0, 2, 3, 1).reshape(bd, B_WIDTH, w_buf)
    b_s = _sattn(qs, ks, vs, cache_kt, cache_vt, t_new)
    k_new_t, v_new_t = _transpose_blocks(
        ks.reshape(bd, t_new * B_WIDTH), vs.reshape(bd, t_new * B_WIDTH), (t_new,),
        (bd, B_WIDTH), lambda t: (0, t), (t_new, B_WIDTH, bd), lambda t: (t, 0, 0))
    k_new = k_new_t.reshape(t_new, HEADS, HEAD_DIM, bd).transpose(3, 0, 1, 2)
    v_new = v_new_t.reshape(t_new, HEADS, HEAD_DIM, bd).transpose(3, 0, 1, 2)
    seqs_per_chunk = CHUNK // t_new
    eye = jnp.eye(seqs_per_chunk, dtype=w_spatial.dtype)
    wsp_s = jax.vmap(lambda w: jnp.kron(eye, w))(w_spatial[:, :t_new, :t_new])
    bsp_s = jnp.tile(b_spatial[:, :t_new], (1, seqs_per_chunk)).reshape(A_GROUPS, CHUNK, 1)
    y_sample = _finish(xs, us, vas, sgas, b_s, sgbs, wsp_s, bsp_s, ga, gb, w_out_bf, fg,
                       tm).reshape(bd, t_new, D_MODEL)

    return (y_prompt, y_sample, k_win, v_win, k_new, v_new, vas.reshape(bd, t_new, A_WIDTH))
```

```python
import functools

import numpy as np
import jax
import jax.numpy as jnp
from jax import lax
from jax.experimental import pallas as pl
from jax.experimental.pallas import tpu as pltpu

D_MODEL = 1024
A_WIDTH = 512
B_WIDTH = 512
A_GROUPS = 4
CHUNK = 128
HEADS = 8
HEAD_DIM = 64
ROT_DIM = 16
ROPE_THETA = 500000.0
CONFIGS = ((128, 1), (512, 4), (2048, 16))
STRIDES_PER_WINDOW = 128
BAND = 128
WIN_MAX = 2048
EPS = 1e-6
LOG2E = 1.4426950408889634
IN_COLS = 3 * A_WIDTH + 4 * B_WIDTH

LANES = 128
HEADS_PER_SLAB = LANES // HEAD_DIM
N_SLABS = B_WIDTH // LANES
VMEM_LIMIT = 48 * 1024 * 1024

F32 = jnp.float32
BF16 = jnp.bfloat16


def _rms(x, g):
    return x * lax.rsqrt(jnp.mean(x * x, axis=-1, keepdims=True) + EPS) * g


def _silu(x):
    return x / (1.0 + jnp.exp(-x))


def _proj_kernel(x_ref, ng_ref, w_ref, lng_ref, lnb_ref, c_ref, sn_ref, sp_ref,
                 u_ref, va_ref, sga_ref, sgb_ref, q_ref, k_ref, v_ref, *, q_scale):
    h = _rms(x_ref[...], ng_ref[...]).astype(BF16)

    def col(g):
        return jnp.dot(h, w_ref[:, g * A_WIDTH:(g + 1) * A_WIDTH], preferred_element_type=F32)

    def rope(z, scale):
        c, sn, sp = c_ref[...], sn_ref[...], sp_ref[...]
        outs = []
        for s in range(N_SLABS):
            zs = z[:, s * LANES:(s + 1) * LANES]
            up = pltpu.roll(zs, LANES - ROT_DIM // 2, 1)
            dn = pltpu.roll(zs, ROT_DIM // 2, 1)
            r = zs * c + up * sn + dn * sp
            outs.append(r * scale if scale != 1.0 else r)
        return jnp.concatenate(outs, axis=1)

    u_ref[...] = jax.nn.gelu(col(0)).astype(u_ref.dtype)
    va = jax.nn.gelu(col(1))
    mu = jnp.mean(va, axis=-1, keepdims=True)
    vc = va - mu
    va = vc * lax.rsqrt(jnp.mean(vc * vc, axis=-1, keepdims=True) + EPS) * lng_ref[...] + lnb_ref[...]
    va_ref[...] = va.astype(va_ref.dtype)
    sga_ref[...] = _silu(col(2)).astype(sga_ref.dtype)
    q_ref[...] = rope(col(3), q_scale)
    k_ref[...] = rope(col(4), 1.0)
    v_ref[...] = col(5)
    sgb_ref[...] = _silu(col(6)).astype(sgb_ref.dtype)


def _proj(x, norm_g, w_bf, ln_g, ln_b, tabs, tab_blocks, va_dtype, tm, q_scale):
    n = x.shape[0]
    steps = n // tm
    row = lambda i: (i, 0)
    fixed = lambda i: (0, 0)
    tab = lambda i: (i % tab_blocks, 0)
    out_shape = [
        jax.ShapeDtypeStruct((n, A_WIDTH), BF16),
        jax.ShapeDtypeStruct((n, A_WIDTH), va_dtype),
        jax.ShapeDtypeStruct((n, A_WIDTH), BF16),
        jax.ShapeDtypeStruct((n, B_WIDTH), BF16),
        jax.ShapeDtypeStruct((n, B_WIDTH), F32),
        jax.ShapeDtypeStruct((n, B_WIDTH), F32),
        jax.ShapeDtypeStruct((n, B_WIDTH), F32),
    ]
    return pl.pallas_call(
        functools.partial(_proj_kernel, q_scale=q_scale),
        grid=(steps,),
        in_specs=[
            pl.BlockSpec((tm, D_MODEL), row),
            pl.BlockSpec((1, D_MODEL), fixed),
            pl.BlockSpec((D_MODEL, IN_COLS), fixed),
            pl.BlockSpec((1, A_WIDTH), fixed),
            pl.BlockSpec((1, A_WIDTH), fixed),
            pl.BlockSpec((tm, LANES), tab),
            pl.BlockSpec((tm, LANES), tab),
            pl.BlockSpec((tm, LANES), tab),
        ],
        out_specs=[pl.BlockSpec((tm, A_WIDTH), row)] * 7,
        out_shape=out_shape,
        compiler_params=pltpu.CompilerParams(
            dimension_semantics=("arbitrary",), vmem_limit_bytes=VMEM_LIMIT),
        name="proj",
    )(x, norm_g, w_bf, ln_g, ln_b, *tabs)


def _rope_tables(pos):
    inv = ROPE_THETA ** (-jnp.arange(0, ROT_DIM, 2, dtype=F32) / ROT_DIM)
    ang = pos.astype(F32)[:, None] * inv[None, :]
    cos, sin = jnp.cos(ang), jnp.sin(ang)
    half = ROT_DIM // 2
    ones = jnp.ones((pos.shape[0], HEAD_DIM - ROT_DIM), F32)
    zeros = jnp.zeros((pos.shape[0], HEAD_DIM - half), F32)
    c = jnp.concatenate([cos, cos, ones], axis=1)
    sn = jnp.concatenate([-sin, zeros], axis=1)
    sp = jnp.concatenate([zeros[:, :half], sin, zeros[:, :HEAD_DIM - ROT_DIM]], axis=1)
    rep = lambda t: jnp.tile(t, (1, HEADS_PER_SLAB))
    return rep(c), rep(sn), rep(sp)


ATT_TILE = WIN_MAX
UNIT_UNROLL = 8


def _attn_kernel(q_ref, k_ref, v_ref, o_ref, osc, lsc, bias_ref):
    seq = q_ref.shape[1]
    lane = lax.broadcasted_iota(jnp.int32, (BAND, LANES), 1)
    head0 = lane < HEAD_DIM
    ii = lax.broadcasted_iota(jnp.int32, (BAND, 2 * BAND), 0)
    jj = lax.broadcasted_iota(jnp.int32, (BAND, 2 * BAND), 1)
    in_window = (jj <= ii + BAND) & (jj >= ii)
    bias_ref[1] = jnp.where(in_window, 0.0, -jnp.inf)
    bias_ref[0] = jnp.where(in_window & (jj >= BAND), 0.0, -jnp.inf)

    def unit(c, dil, cls, m0, t):
        q = q_ref[0, pl.ds(cls + dil * m0, BAND, stride=dil), :]
        mp = jnp.maximum(m0 - BAND, 0)

        def two_blocks(ref):
            prev = ref[0, pl.ds(cls + dil * mp, BAND, stride=dil), :]
            cur = ref[0, pl.ds(cls + dil * m0, BAND, stride=dil), :]
            return jnp.concatenate([prev, cur], axis=0).astype(BF16)

        kk = two_blocks(k_ref)
        vv1 = jnp.concatenate([two_blocks(v_ref), jnp.ones((2 * BAND, LANES), BF16)], axis=1)
        bias = bias_ref[jnp.where(m0 > 0, 1, 0)]
        pvs, ms = [], []
        for hmask in (head0, jnp.logical_not(head0)):
            qh = jnp.where(hmask, q, 0.0).astype(BF16)
            s = lax.dot_general(qh, kk, (((1,), (1,)), ((), ())), preferred_element_type=F32)
            s = s + bias
            m = jnp.max(s, axis=1, keepdims=True)
            p = jnp.exp2(s - m)
            pvs.append(jnp.dot(p.astype(BF16), vv1, preferred_element_type=F32))
            ms.append(jnp.broadcast_to(m, (BAND, LANES)))
        pv = jnp.where(head0, pvs[0][:, :LANES], pvs[1][:, :LANES])
        l = jnp.where(head0, pvs[0][:, LANES:], pvs[1][:, LANES:])
        row0 = cls + dil * m0 - t * ATT_TILE
        osc[c, pl.ds(row0, BAND, stride=dil), :] = pv / l
        lsc[c, pl.ds(row0, BAND, stride=dil), :] = jnp.where(head0, ms[0], ms[1]) + jnp.log2(l)

    units = ATT_TILE // BAND

    def tile_body(t, carry):
        for c, (_, dil) in enumerate(CONFIGS):
            blocks = units // dil

            def body(u, carry2, c=c, dil=dil, blocks=blocks):
                cls = u // blocks
                nb = u - cls * blocks
                unit(c, dil, cls, t * (ATT_TILE // dil) + nb * BAND, t)
                return carry2

            lax.fori_loop(0, units, body, 0, unroll=UNIT_UNROLL)

        rows = 256
        for r in range(ATT_TILE // rows):
            sl = pl.ds(r * rows, rows)
            ls = [lsc[c, sl, :] for c in range(len(CONFIGS))]
            mx = jnp.maximum(jnp.maximum(ls[0], ls[1]), ls[2])
            ws = [jnp.exp2(l - mx) for l in ls]
            num = ws[0] * osc[0, sl, :] + ws[1] * osc[1, sl, :] + ws[2] * osc[2, sl, :]
            o_ref[0, pl.ds(pl.multiple_of(t * ATT_TILE + r * rows, rows), rows), :] = (
                num / (ws[0] + ws[1] + ws[2])).astype(o_ref.dtype)
        return carry

    lax.fori_loop(0, seq // ATT_TILE, tile_body, 0)


def _attn(q, k, v):
    bn, seq, _ = q.shape
    spec = pl.BlockSpec((1, seq, LANES), lambda b, s: (b, 0, s))
    return pl.pallas_call(
        _attn_kernel,
        grid=(bn, N_SLABS),
        in_specs=[spec, spec, spec],
        out_specs=spec,
        out_shape=jax.ShapeDtypeStruct((bn, seq, B_WIDTH), BF16),
        scratch_shapes=[pltpu.VMEM((len(CONFIGS), ATT_TILE, LANES), F32),
                        pltpu.VMEM((len(CONFIGS), ATT_TILE, LANES), F32),
                        pltpu.VMEM((2, BAND, 2 * BAND), F32)],
        compiler_params=pltpu.CompilerParams(
            dimension_semantics=("arbitrary", "arbitrary"), vmem_limit_bytes=VMEM_LIMIT),
        name="attn",
    )(q, k, v)


NEW_PAD = 128


def _sample_counts(w_buf, t_new):
    tq = np.arange(t_new)[:, None]

    def count(idx, exists):
        dist = w_buf + tq - idx[None, :]
        cnt = np.zeros(dist.shape, np.float32)
        for win, dil in CONFIGS:
            cnt += ((dist >= 0) & (dist <= win) & (dist % dil == 0) & exists[None, :])
        return np.tile(cnt, (HEADS, 1))

    old = np.arange(w_buf)
    new = np.arange(w_buf, w_buf + NEW_PAD)
    return count(old, old >= 0), count(new, new < w_buf + t_new)


def _sattn_kernel(q_ref, kn_ref, vn_ref, kt_ref, vt_ref, cnt_old_ref, cnt_new_ref, o_ref):
    t_new = q_ref.shape[0]
    rows = HEADS * t_new
    qrep = jnp.concatenate([q_ref[...]] * HEADS, axis=0)
    rh = lax.broadcasted_iota(jnp.int32, (rows, B_WIDTH), 0) // t_new
    lh = lax.broadcasted_iota(jnp.int32, (rows, B_WIDTH), 1) // HEAD_DIM
    own = rh == lh
    qbd = jnp.where(own, qrep, 0.0).astype(BF16)
    pad = jnp.zeros((NEW_PAD - t_new, B_WIDTH), F32)
    kn = jnp.concatenate([kn_ref[...], pad], axis=0).astype(BF16)
    vn = jnp.concatenate([vn_ref[...], pad], axis=0).astype(BF16)
    cnt_old, cnt_new = cnt_old_ref[...], cnt_new_ref[...]
    s_old = jnp.dot(qbd, kt_ref[0].astype(BF16), preferred_element_type=F32)
    s_new = lax.dot_general(qbd, kn, (((1,), (1,)), ((), ())), preferred_element_type=F32)
    s_old = jnp.where(cnt_old > 0.0, s_old, -jnp.inf)
    s_new = jnp.where(cnt_new > 0.0, s_new, -jnp.inf)
    m = jnp.maximum(jnp.max(s_old, axis=1, keepdims=True), jnp.max(s_new, axis=1, keepdims=True))
    p_old = jnp.exp(s_old - m) * cnt_old
    p_new = jnp.exp(s_new - m) * cnt_new
    l = jnp.sum(p_old, axis=1, keepdims=True) + jnp.sum(p_new, axis=1, keepdims=True)
    o = lax.dot_general(p_old.astype(BF16), vt_ref[0].astype(BF16), (((1,), (1,)), ((), ())),
                        preferred_element_type=F32)
    o = (o + jnp.dot(p_new.astype(BF16), vn, preferred_element_type=F32)) / l
    o = jnp.where(own, o, 0.0)
    acc = o[0:t_new]
    for h in range(1, HEADS):
        acc = acc + o[h * t_new:(h + 1) * t_new]
    o_ref[...] = acc.astype(o_ref.dtype)


def _sattn(q, k_new, v_new, cache_kt, cache_vt, t_new):
    n = q.shape[0]
    bd, _, w_buf = cache_kt.shape
    cnt_old, cnt_new = (jnp.asarray(c) for c in _sample_counts(w_buf, t_new))
    new_spec = pl.BlockSpec((t_new, B_WIDTH), lambda b: (b, 0))
    buf_spec = pl.BlockSpec((1, B_WIDTH, w_buf), lambda b: (b, 0, 0))
    return pl.pallas_call(
        _sattn_kernel,
        grid=(bd,),
        in_specs=[new_spec, new_spec, new_spec, buf_spec, buf_spec,
                  pl.BlockSpec(cnt_old.shape, lambda b: (0, 0)),
                  pl.BlockSpec(cnt_new.shape, lambda b: (0, 0))],
        out_specs=new_spec,
        out_shape=jax.ShapeDtypeStruct((n, B_WIDTH), BF16),
        compiler_params=pltpu.CompilerParams(
            dimension_semantics=("arbitrary",), vmem_limit_bytes=VMEM_LIMIT),
        name="sattn",
    )(q, k_new, v_new, cache_kt, cache_vt, cnt_old, cnt_new)


def _transpose_kernel(k_ref, v_ref, kt_ref, vt_ref):
    kt_ref[0] = k_ref[...].T
    vt_ref[0] = v_ref[...].T


def _transpose_blocks(k, v, grid, src_block, src_map, out_dims, dst_map):
    src = pl.BlockSpec(src_block, src_map)
    dst = pl.BlockSpec((1, src_block[1], src_block[0]), dst_map)
    shape = jax.ShapeDtypeStruct(out_dims, F32)
    return pl.pallas_call(
        _transpose_kernel, grid=grid, in_specs=[src, src], out_specs=[dst, dst],
        out_shape=[shape, shape],
        compiler_params=pltpu.CompilerParams(dimension_semantics=("arbitrary",) * len(grid)),
        name="transpose",
    )(k, v)


def _finish_kernel(x_ref, u_ref, va_ref, sga_ref, b_ref, sgb_ref, wsp_ref, bsp_ref,
                   gna_ref, gnb_ref, wout_ref, fg_ref, y_ref):
    tm = x_ref.shape[0]
    nc = tm // CHUNK
    ri = lax.broadcasted_iota(jnp.int32, (CHUNK, CHUNK), 0)
    ci = lax.broadcasted_iota(jnp.int32, (CHUNK, CHUNK), 1)
    causal = ri >= ci
    va = va_ref[...].astype(BF16)
    u = u_ref[...].astype(F32)
    a_cols = []
    for g in range(A_GROUPS):
        wm = jnp.where(causal, wsp_ref[g], 0.0).astype(BF16)
        lanes = slice(g * CHUNK, (g + 1) * CHUNK)
        rhs = jnp.concatenate([va[c * CHUNK:(c + 1) * CHUNK, lanes] for c in range(nc)], axis=1)
        s = jnp.dot(wm, rhs, preferred_element_type=F32) + bsp_ref[g]
        s = jnp.concatenate([s[:, c * CHUNK:(c + 1) * CHUNK] for c in range(nc)], axis=0)
        a_cols.append(u[:, lanes] * s)
    a = jnp.concatenate(a_cols, axis=1)
    ha = _rms(a * sga_ref[...].astype(F32), gna_ref[...])
    hb = _rms(b_ref[...].astype(F32) * sgb_ref[...].astype(F32), gnb_ref[...])
    hcat = jnp.concatenate([ha, hb], axis=1).astype(BF16)
    y = x_ref[...] + jnp.dot(hcat, wout_ref[...], preferred_element_type=F32)
    y_ref[...] = _rms(y, fg_ref[...])


def _finish(x, u, va, sga, b, sgb, wsp, bsp, gn_a, gn_b, wout_bf, final_g, tm):
    n = x.shape[0]
    row = lambda i: (i, 0)
    fixed2 = lambda i: (0, 0)
    fixed3 = lambda i: (0, 0, 0)
    half = pl.BlockSpec((tm, A_WIDTH), row)
    return pl.pallas_call(
        _finish_kernel,
        grid=(n // tm,),
        in_specs=[
            pl.BlockSpec((tm, D_MODEL), row), half, half, half, half, half,
            pl.BlockSpec((A_GROUPS, CHUNK, CHUNK), fixed3),
            pl.BlockSpec((A_GROUPS, CHUNK, 1), fixed3),
            pl.BlockSpec((1, A_WIDTH), fixed2),
            pl.BlockSpec((1, B_WIDTH), fixed2),
            pl.BlockSpec((D_MODEL, D_MODEL), fixed2),
            pl.BlockSpec((1, D_MODEL), fixed2),
        ],
        out_specs=pl.BlockSpec((tm, D_MODEL), row),
        out_shape=jax.ShapeDtypeStruct((n, D_MODEL), F32),
        compiler_params=pltpu.CompilerParams(
            dimension_semantics=("arbitrary",), vmem_limit_bytes=VMEM_LIMIT),
        name="finish",
    )(x, u, va, sga, b, sgb, wsp, bsp, gn_a, gn_b, wout_bf, final_g)


def kernel(x_prompt, x_sample, cache_k_win, cache_v_win, norm_g, w_in, ln_v_g, ln_v_b,
           w_spatial, b_spatial, gn_a, gn_b, w_out, final_g):
    bp, seq, _ = x_prompt.shape
    bd, t_new, _ = x_sample.shape
    past_len = seq
    w_in_bf = w_in.astype(BF16)
    w_out_bf = w_out.astype(BF16)
    ng = norm_g.reshape(1, D_MODEL)
    lg, lb = ln_v_g.reshape(1, A_WIDTH), ln_v_b.reshape(1, A_WIDTH)
    ga, gb = gn_a.reshape(1, A_WIDTH), gn_b.reshape(1, B_WIDTH)
    fg = final_g.reshape(1, D_MODEL)
    tm = 512
    tm_finish = 1024

    xp = x_prompt.reshape(bp * seq, D_MODEL)
    tabs_p = _rope_tables(jnp.arange(seq))
    u, va, sga, sgb, q, k, v = _proj(xp, ng, w_in_bf, lg, lb, tabs_p, seq // tm, BF16, tm,
                                     LOG2E * HEAD_DIM ** -0.5)
    b_attn = _attn(q.reshape(bp, seq, B_WIDTH), k.reshape(bp, seq, B_WIDTH),
                   v.reshape(bp, seq, B_WIDTH))
    bsp = b_spatial.reshape(A_GROUPS, CHUNK, 1)
    y_prompt = _finish(xp, u, va, sga, b_attn.reshape(bp * seq, B_WIDTH), sgb, w_spatial, bsp,
                       ga, gb, w_out_bf, fg, tm_finish).reshape(bp, seq, D_MODEL)
    w_keep = min(WIN_MAX, seq)
    per_seq, first = seq // tm, (seq - w_keep) // tm
    k_win_t, v_win_t = _transpose_blocks(
        k, v, (bp, w_keep // tm), (tm, B_WIDTH), lambda b, j: (b * per_seq + first + j, 0),
        (bp, B_WIDTH, w_keep), lambda b, j: (b, 0, j))
    k_win = k_win_t.reshape(bp, HEADS, HEAD_DIM, w_keep).transpose(0, 3, 1, 2)
    v_win = v_win_t.reshape(bp, HEADS, HEAD_DIM, w_keep).transpose(0, 3, 1, 2)

    xs = x_sample.reshape(bd * t_new, D_MODEL)
    pos_s = past_len + jnp.tile(jnp.arange(t_new), bd)
    tabs_s = _rope_tables(pos_s)
    n_s = bd * t_new
    us, vas, sgas, sgbs, qs, ks, vs = _proj(xs, ng, w_in_bf, lg, lb, tabs_s, n_s // tm, F32, tm,
                                            HEAD_DIM ** -0.5)
    w_buf = cache_k_win.shape[1]
    cache_kt = cache_k_win.transpose(0, 2, 3, 1).reshape(bd, B_WIDTH, w_buf)
    cache_vt = cache_v_win.transpose(0, 2, 3, 1).reshape(bd, B_WIDTH, w_buf)
    b_s = _sattn(qs, ks, vs, cache_kt, cache_vt, t_new)
    k_new_t, v_new_t = _transpose_blocks(
        ks.reshape(bd, t_new * B_WIDTH), vs.reshape(bd, t_new * B_WIDTH), (t_new,),
        (bd, B_WIDTH), lambda t: (0, t), (t_new, B_WIDTH, bd), lambda t: (t, 0, 0))
    k_new = k_new_t.reshape(t_new, HEADS, HEAD_DIM, bd).transpose(3, 0, 1, 2)
    v_new = v_new_t.reshape(t_new, HEADS, HEAD_DIM, bd).transpose(3, 0, 1, 2)
    seqs_per_chunk = CHUNK // t_new
    eye = jnp.eye(seqs_per_chunk, dtype=w_spatial.dtype)
    wsp_s = jax.vmap(lambda w: jnp.kron(eye, w))(w_spatial[:, :t_new, :t_new])
    bsp_s = jnp.tile(b_spatial[:, :t_new], (1, seqs_per_chunk)).reshape(A_GROUPS, CHUNK, 1)
    y_sample = _finish(xs, us, vas, sgas, b_s, sgbs, wsp_s, bsp_s, ga, gb, w_out_bf, fg,
                       tm_finish).reshape(bd, t_new, D_MODEL)

    return (y_prompt, y_sample, k_win, v_win, k_new, v_new, vas.reshape(bd, t_new, A_WIDTH))
```

```python
import functools

import numpy as np
import jax
import jax.numpy as jnp
from jax import lax
from jax.experimental import pallas as pl
from jax.experimental.pallas import tpu as pltpu

D_MODEL = 1024
A_WIDTH = 512
B_WIDTH = 512
A_GROUPS = 4
CHUNK = 128
HEADS = 8
HEAD_DIM = 64
ROT_DIM = 16
ROPE_THETA = 500000.0
CONFIGS = ((128, 1), (512, 4), (2048, 16))
BAND = 128
WIN_MAX = 2048
EPS = 1e-6
LOG2E = 1.4426950408889634
IN_COLS = 3 * A_WIDTH + 4 * B_WIDTH

LANES = 128
HEADS_PER_SLAB = LANES // HEAD_DIM
N_SLABS = B_WIDTH // LANES
VMEM_LIMIT = 56 * 1024 * 1024

F32 = jnp.float32
BF16 = jnp.bfloat16


def _rms(x, g):
    return x * lax.rsqrt(jnp.mean(x * x, axis=-1, keepdims=True) + EPS) * g


def _silu(x):
    return x / (1.0 + jnp.exp(-x))


def _project_tile(x_ref, ng_ref, w_ref, lng_ref, lnb_ref, c_ref, sn_ref, sp_ref,
                  u_ref, va_ref, sga_ref, sgb_ref, q_ref, k_ref, v_ref, q_scale):
    h = _rms(x_ref[...], ng_ref[...]).astype(BF16)

    def col(g):
        return jnp.dot(h, w_ref[:, g * A_WIDTH:(g + 1) * A_WIDTH], preferred_element_type=F32)

    def rope(z, scale):
        c, sn, sp = c_ref[...], sn_ref[...], sp_ref[...]
        outs = []
        for s in range(N_SLABS):
            zs = z[:, s * LANES:(s + 1) * LANES]
            up = pltpu.roll(zs, LANES - ROT_DIM // 2, 1)
            dn = pltpu.roll(zs, ROT_DIM // 2, 1)
            r = zs * c + up * sn + dn * sp
            outs.append(r * scale if scale != 1.0 else r)
        return jnp.concatenate(outs, axis=1)

    u_ref[...] = jax.nn.gelu(col(0)).astype(u_ref.dtype)
    va = jax.nn.gelu(col(1))
    mu = jnp.mean(va, axis=-1, keepdims=True)
    vc = va - mu
    va = vc * lax.rsqrt(jnp.mean(vc * vc, axis=-1, keepdims=True) + EPS) * lng_ref[...] + lnb_ref[...]
    va_ref[...] = va.astype(va_ref.dtype)
    sga_ref[...] = _silu(col(2)).astype(sga_ref.dtype)
    q_ref[...] = rope(col(3), q_scale)
    k = rope(col(4), 1.0)
    v = col(5)
    k_ref[...] = k
    v_ref[...] = v
    sgb_ref[...] = _silu(col(6)).astype(sgb_ref.dtype)
    return k, v


def _proj_sample_kernel(x_ref, ng_ref, w_ref, lng_ref, lnb_ref, c_ref, sn_ref, sp_ref,
                        u_ref, va_ref, sga_ref, sgb_ref, q_ref, k_ref, v_ref, *, q_scale):
    _project_tile(x_ref, ng_ref, w_ref, lng_ref, lnb_ref, c_ref, sn_ref, sp_ref,
                  u_ref, va_ref, sga_ref, sgb_ref, q_ref, k_ref, v_ref, q_scale)


def _proj_prompt_kernel(x_ref, ng_ref, w_ref, lng_ref, lnb_ref, c_ref, sn_ref, sp_ref,
                        qs_ref, kn_ref, vn_ref, ktc_ref, vtc_ref, cnt_old_ref, cnt_new_ref,
                        u_ref, va_ref, sga_ref, sgb_ref, q_ref, k_ref, v_ref, kt_ref, vt_ref, bs_ref,
                        *, q_scale):
    _sample_attention(qs_ref, kn_ref, vn_ref, ktc_ref, vtc_ref, cnt_old_ref, cnt_new_ref, bs_ref)
    k, v = _project_tile(x_ref, ng_ref, w_ref, lng_ref, lnb_ref, c_ref, sn_ref, sp_ref,
                         u_ref, va_ref, sga_ref, sgb_ref, q_ref, k_ref, v_ref, q_scale)
    kt_ref[0] = k.T
    vt_ref[0] = v.T


def _proj_in_specs(tm, row, tab, fixed):
    return [
        pl.BlockSpec((tm, D_MODEL), row),
        pl.BlockSpec((1, D_MODEL), fixed),
        pl.BlockSpec((D_MODEL, IN_COLS), fixed),
        pl.BlockSpec((1, A_WIDTH), fixed),
        pl.BlockSpec((1, A_WIDTH), fixed),
        pl.BlockSpec((tm, LANES), tab),
        pl.BlockSpec((tm, LANES), tab),
        pl.BlockSpec((tm, LANES), tab),
    ]


def _proj_sample(x, norm_g, w_bf, ln_g, ln_b, tabs, tm, q_scale):
    n = x.shape[0]
    row = lambda i: (i, 0)
    fixed = lambda i: (0, 0)
    half = lambda dt: jax.ShapeDtypeStruct((n, A_WIDTH), dt)
    return pl.pallas_call(
        functools.partial(_proj_sample_kernel, q_scale=q_scale),
        grid=(n // tm,),
        in_specs=_proj_in_specs(tm, row, row, fixed),
        out_specs=[pl.BlockSpec((tm, A_WIDTH), row)] * 7,
        out_shape=[half(BF16), half(F32), half(BF16), half(BF16), half(F32), half(F32), half(F32)],
        compiler_params=pltpu.CompilerParams(
            dimension_semantics=("arbitrary",), vmem_limit_bytes=VMEM_LIMIT),
        name="proj_sample",
    )(x, norm_g, w_bf, ln_g, ln_b, *tabs)


def _proj_prompt(x, bn, seq, norm_g, w_bf, ln_g, ln_b, tabs, q_scale, keep,
                 qs, k_new, v_new, cache_kt, cache_vt, t_new):
    n = bn * seq
    bd, _, w_buf = cache_kt.shape
    assert n % bd == 0, "one sample sequence per prompt tile"
    tm = n // bd
    assert tm % CHUNK == 0 and seq % tm == 0 and keep % tm == 0
    per_seq = seq // tm
    first = (seq - keep) // tm
    cnt_old, cnt_new = (jnp.asarray(c) for c in _sample_counts(w_buf, t_new))
    row = lambda b, j: (b * per_seq + j, 0)
    tab = lambda b, j: (j, 0)
    fixed = lambda b, j: (0, 0)
    new_spec = pl.BlockSpec((t_new, B_WIDTH), row)
    buf_spec = pl.BlockSpec((1, B_WIDTH, w_buf), lambda b, j: (b * per_seq + j, 0, 0))
    win_spec = pl.BlockSpec((1, B_WIDTH, tm), lambda b, j: (b, 0, jnp.maximum(j - first, 0)))
    half = lambda dt: jax.ShapeDtypeStruct((n, A_WIDTH), dt)
    win = jax.ShapeDtypeStruct((bn, B_WIDTH, keep), F32)
    return pl.pallas_call(
        functools.partial(_proj_prompt_kernel, q_scale=q_scale),
        grid=(bn, per_seq),
        in_specs=_proj_in_specs(tm, row, tab, fixed) + [
            new_spec, new_spec, new_spec, buf_spec, buf_spec,
            pl.BlockSpec(cnt_old.shape, fixed), pl.BlockSpec(cnt_new.shape, fixed)],
        out_specs=[pl.BlockSpec((tm, A_WIDTH), row)] * 7 + [win_spec, win_spec, new_spec],
        out_shape=[half(BF16)] * 4 + [half(F32)] * 3 + [win, win,
                   jax.ShapeDtypeStruct((bd * t_new, B_WIDTH), BF16)],
        compiler_params=pltpu.CompilerParams(
            dimension_semantics=("arbitrary", "arbitrary"), vmem_limit_bytes=VMEM_LIMIT),
        name="proj_prompt",
    )(x, norm_g, w_bf, ln_g, ln_b, *tabs, qs, k_new, v_new, cache_kt, cache_vt, cnt_old, cnt_new)


def _rope_tables(pos):
    inv = ROPE_THETA ** (-jnp.arange(0, ROT_DIM, 2, dtype=F32) / ROT_DIM)
    ang = pos.astype(F32)[:, None] * inv[None, :]
    cos, sin = jnp.cos(ang), jnp.sin(ang)
    half = ROT_DIM // 2
    ones = jnp.ones((pos.shape[0], HEAD_DIM - ROT_DIM), F32)
    zeros = jnp.zeros((pos.shape[0], HEAD_DIM - half), F32)
    c = jnp.concatenate([cos, cos, ones], axis=1)
    sn = jnp.concatenate([-sin, zeros], axis=1)
    sp = jnp.concatenate([zeros[:, :half], sin, zeros[:, :HEAD_DIM - ROT_DIM]], axis=1)
    rep = lambda t: jnp.tile(t, (1, HEADS_PER_SLAB))
    return rep(c), rep(sn), rep(sp)


ATT_TILE = WIN_MAX
UNIT_UNROLL = 8


def _attn_kernel(q_ref, k_ref, v_ref, o_ref, osc, lsc, bias_ref):
    seq = q_ref.shape[1]
    head0 = lax.broadcasted_iota(jnp.int32, (BAND, LANES), 1) < HEAD_DIM
    ii = lax.broadcasted_iota(jnp.int32, (2 * BAND, 2 * BAND), 0) % BAND
    jj = lax.broadcasted_iota(jnp.int32, (2 * BAND, 2 * BAND), 1)
    in_window = (jj <= ii + BAND) & (jj >= ii)
    bias_ref[1] = jnp.where(in_window, 0.0, -jnp.inf)
    bias_ref[0] = jnp.where(in_window & (jj >= BAND), 0.0, -jnp.inf)
    ones = jnp.ones((2 * BAND, LANES), BF16)

    def unit(c, dil, cls, m0, t):
        mp = jnp.maximum(m0 - BAND, 0)

        def two_blocks(ref):
            prev = ref[0, pl.ds(cls + dil * mp, BAND, stride=dil), :]
            cur = ref[0, pl.ds(cls + dil * m0, BAND, stride=dil), :]
            return jnp.concatenate([prev, cur], axis=0).astype(BF16)

        q = q_ref[0, pl.ds(cls + dil * m0, BAND, stride=dil), :]
        kk = two_blocks(k_ref)
        vv1 = jnp.concatenate([two_blocks(v_ref), ones], axis=1)
        zero = jnp.zeros_like(q)
        qs = jnp.concatenate([jnp.where(head0, q, zero), jnp.where(head0, zero, q)], axis=0)
        s = lax.dot_general(qs.astype(BF16), kk, (((1,), (1,)), ((), ())),
                            preferred_element_type=F32)
        s = s + bias_ref[jnp.where(m0 > 0, 1, 0)]
        m = jnp.max(s, axis=1, keepdims=True)
        p = jnp.exp2(s - m)
        pv = jnp.dot(p.astype(BF16), vv1, preferred_element_type=F32)
        o = jnp.where(head0, pv[:BAND, :LANES], pv[BAND:, :LANES])
        l = jnp.where(head0, pv[:BAND, LANES:], pv[BAND:, LANES:])
        mm = jnp.where(head0, jnp.broadcast_to(m[:BAND], (BAND, LANES)),
                       jnp.broadcast_to(m[BAND:], (BAND, LANES)))
        row0 = cls + dil * m0 - t * ATT_TILE
        osc[c, pl.ds(row0, BAND, stride=dil), :] = o / l
        lsc[c, pl.ds(row0, BAND, stride=dil), :] = mm + jnp.log2(l)

    units = ATT_TILE // BAND

    def tile_body(t, carry):
        for c, (_, dil) in enumerate(CONFIGS):
            blocks = units // dil

            def body(u, carry2, c=c, dil=dil, blocks=blocks):
                cls = u // blocks
                nb = u - cls * blocks
                unit(c, dil, cls, t * (ATT_TILE // dil) + nb * BAND, t)
                return carry2

            lax.fori_loop(0, units, body, 0, unroll=UNIT_UNROLL)

        rows_per_step = 256
        for r in range(ATT_TILE // rows_per_step):
            sl = pl.ds(r * rows_per_step, rows_per_step)
            ls = [lsc[c, sl, :] for c in range(len(CONFIGS))]
            mx = jnp.maximum(jnp.maximum(ls[0], ls[1]), ls[2])
            ws = [jnp.exp2(l - mx) for l in ls]
            num = ws[0] * osc[0, sl, :] + ws[1] * osc[1, sl, :] + ws[2] * osc[2, sl, :]
            dst = pl.ds(pl.multiple_of(t * ATT_TILE + r * rows_per_step, rows_per_step), rows_per_step)
            o_ref[0, dst, :] = (num / (ws[0] + ws[1] + ws[2])).astype(o_ref.dtype)
        return carry

    lax.fori_loop(0, seq // ATT_TILE, tile_body, 0)


def _attn(q, k, v):
    bn, seq, _ = q.shape
    spec = pl.BlockSpec((1, seq, LANES), lambda b, s: (b, 0, s))
    return pl.pallas_call(
        _attn_kernel,
        grid=(bn, N_SLABS),
        in_specs=[spec, spec, spec],
        out_specs=spec,
        out_shape=jax.ShapeDtypeStruct((bn, seq, B_WIDTH), BF16),
        scratch_shapes=[pltpu.VMEM((len(CONFIGS), ATT_TILE, LANES), F32),
                        pltpu.VMEM((len(CONFIGS), ATT_TILE, LANES), F32),
                        pltpu.VMEM((2, 2 * BAND, 2 * BAND), F32)],
        compiler_params=pltpu.CompilerParams(
            dimension_semantics=("arbitrary", "arbitrary"), vmem_limit_bytes=VMEM_LIMIT),
        name="attn",
    )(q, k, v)


NEW_PAD = 128


def _sample_counts(w_buf, t_new):
    tq = np.arange(t_new)[:, None]

    def count(idx, exists):
        dist = w_buf + tq - idx[None, :]
        cnt = np.zeros(dist.shape, np.float32)
        for win, dil in CONFIGS:
            cnt += ((dist >= 0) & (dist <= win) & (dist % dil == 0) & exists[None, :])
        return np.tile(cnt, (HEADS, 1))

    old = np.arange(w_buf)
    new = np.arange(w_buf, w_buf + NEW_PAD)
    return count(old, old >= 0), count(new, new < w_buf + t_new)


def _sample_attention(q_ref, kn_ref, vn_ref, kt_ref, vt_ref, cnt_old_ref, cnt_new_ref, o_ref):
    t_new = q_ref.shape[0]
    rows = HEADS * t_new
    qrep = jnp.concatenate([q_ref[...]] * HEADS, axis=0)
    rh = lax.broadcasted_iota(jnp.int32, (rows, B_WIDTH), 0) // t_new
    lh = lax.broadcasted_iota(jnp.int32, (rows, B_WIDTH), 1) // HEAD_DIM
    own = rh == lh
    qbd = jnp.where(own, qrep, 0.0).astype(BF16)
    pad = jnp.zeros((NEW_PAD - t_new, B_WIDTH), F32)
    kn = jnp.concatenate([kn_ref[...], pad], axis=0).astype(BF16)
    vn = jnp.concatenate([vn_ref[...], pad], axis=0).astype(BF16)
    cnt_old, cnt_new = cnt_old_ref[...], cnt_new_ref[...]
    s_old = jnp.dot(qbd, kt_ref[0].astype(BF16), preferred_element_type=F32)
    s_new = lax.dot_general(qbd, kn, (((1,), (1,)), ((), ())), preferred_element_type=F32)
    s_old = jnp.where(cnt_old > 0.0, s_old, -jnp.inf)
    s_new = jnp.where(cnt_new > 0.0, s_new, -jnp.inf)
    m = jnp.maximum(jnp.max(s_old, axis=1, keepdims=True), jnp.max(s_new, axis=1, keepdims=True))
    p_old = jnp.exp(s_old - m) * cnt_old
    p_new = jnp.exp(s_new - m) * cnt_new
    l = jnp.sum(p_old, axis=1, keepdims=True) + jnp.sum(p_new, axis=1, keepdims=True)
    o = lax.dot_general(p_old.astype(BF16), vt_ref[0].astype(BF16), (((1,), (1,)), ((), ())),
                        preferred_element_type=F32)
    o = (o + jnp.dot(p_new.astype(BF16), vn, preferred_element_type=F32)) / l
    o = jnp.where(own, o, 0.0)
    acc = o[0:t_new]
    for h in range(1, HEADS):
        acc = acc + o[h * t_new:(h + 1) * t_new]
    o_ref[...] = acc.astype(o_ref.dtype)


def _transpose_kernel(k_ref, v_ref, kt_ref, vt_ref):
    kt_ref[0] = k_ref[...].T
    vt_ref[0] = v_ref[...].T


def _transpose_blocks(k, v, grid, src_block, src_map, out_dims, dst_map):
    src = pl.BlockSpec(src_block, src_map)
    dst = pl.BlockSpec((1, src_block[1], src_block[0]), dst_map)
    shape = jax.ShapeDtypeStruct(out_dims, F32)
    return pl.pallas_call(
        _transpose_kernel, grid=grid, in_specs=[src, src], out_specs=[dst, dst],
        out_shape=[shape, shape],
        compiler_params=pltpu.CompilerParams(dimension_semantics=("arbitrary",) * len(grid)),
        name="transpose",
    )(k, v)


def _finish_kernel(x_ref, u_ref, va_ref, sga_ref, b_ref, sgb_ref, wsp_ref, bsp_ref,
                   gna_ref, gnb_ref, wout_ref, fg_ref, y_ref):
    tm = x_ref.shape[0]
    nc = tm // CHUNK
    ri = lax.broadcasted_iota(jnp.int32, (CHUNK, CHUNK), 0)
    ci = lax.broadcasted_iota(jnp.int32, (CHUNK, CHUNK), 1)
    causal = ri >= ci
    va = va_ref[...].astype(BF16)
    u = u_ref[...].astype(F32)
    a_cols = []
    for g in range(A_GROUPS):
        wm = jnp.where(causal, wsp_ref[g], 0.0).astype(BF16)
        lanes = slice(g * CHUNK, (g + 1) * CHUNK)
        rhs = jnp.concatenate([va[c * CHUNK:(c + 1) * CHUNK, lanes] for c in range(nc)], axis=1)
        s = jnp.dot(wm, rhs, preferred_element_type=F32) + bsp_ref[g]
        s = jnp.concatenate([s[:, c * CHUNK:(c + 1) * CHUNK] for c in range(nc)], axis=0)
        a_cols.append(u[:, lanes] * s)
    a = jnp.concatenate(a_cols, axis=1)
    ha = _rms(a * sga_ref[...].astype(F32), gna_ref[...])
    hb = _rms(b_ref[...].astype(F32) * sgb_ref[...].astype(F32), gnb_ref[...])
    hcat = jnp.concatenate([ha, hb], axis=1).astype(BF16)
    y = x_ref[...] + jnp.dot(hcat, wout_ref[...], preferred_element_type=F32)
    y_ref[...] = _rms(y, fg_ref[...])


def _finish(x, u, va, sga, b, sgb, wsp, bsp, gn_a, gn_b, wout_bf, final_g, tm):
    n = x.shape[0]
    row = lambda i: (i, 0)
    fixed2 = lambda i: (0, 0)
    fixed3 = lambda i: (0, 0, 0)
    half = pl.BlockSpec((tm, A_WIDTH), row)
    return pl.pallas_call(
        _finish_kernel,
        grid=(n // tm,),
        in_specs=[
            pl.BlockSpec((tm, D_MODEL), row), half, half, half, half, half,
            pl.BlockSpec((A_GROUPS, CHUNK, CHUNK), fixed3),
            pl.BlockSpec((A_GROUPS, CHUNK, 1), fixed3),
            pl.BlockSpec((1, A_WIDTH), fixed2),
            pl.BlockSpec((1, B_WIDTH), fixed2),
            pl.BlockSpec((D_MODEL, D_MODEL), fixed2),
            pl.BlockSpec((1, D_MODEL), fixed2),
        ],
        out_specs=pl.BlockSpec((tm, D_MODEL), row),
        out_shape=jax.ShapeDtypeStruct((n, D_MODEL), F32),
        compiler_params=pltpu.CompilerParams(
            dimension_semantics=("arbitrary",), vmem_limit_bytes=VMEM_LIMIT),
        name="finish",
    )(x, u, va, sga, b, sgb, wsp, bsp, gn_a, gn_b, wout_bf, final_g)


def kernel(x_prompt, x_sample, cache_k_win, cache_v_win, norm_g, w_in, ln_v_g, ln_v_b,
           w_spatial, b_spatial, gn_a, gn_b, w_out, final_g):
    bp, seq, _ = x_prompt.shape
    bd, t_new, _ = x_sample.shape
    past_len = seq
    w_in_bf = w_in.astype(BF16)
    w_out_bf = w_out.astype(BF16)
    ng = norm_g.reshape(1, D_MODEL)
    lg, lb = ln_v_g.reshape(1, A_WIDTH), ln_v_b.reshape(1, A_WIDTH)
    ga, gb = gn_a.reshape(1, A_WIDTH), gn_b.reshape(1, B_WIDTH)
    fg = final_g.reshape(1, D_MODEL)
    tm_sample = 512
    tm_finish = 1024

    xs = x_sample.reshape(bd * t_new, D_MODEL)
    tabs_s = _rope_tables(past_len + jnp.tile(jnp.arange(t_new), bd))
    us, vas, sgas, sgbs, qs, ks, vs = _proj_sample(xs, ng, w_in_bf, lg, lb, tabs_s, tm_sample,
                                                   HEAD_DIM ** -0.5)

    w_buf = cache_k_win.shape[1]
    cache_kt = cache_k_win.transpose(0, 2, 3, 1).reshape(bd, B_WIDTH, w_buf)
    cache_vt = cache_v_win.transpose(0, 2, 3, 1).reshape(bd, B_WIDTH, w_buf)
    xp = x_prompt.reshape(bp * seq, D_MODEL)
    tabs_p = _rope_tables(jnp.arange(seq))
    w_keep = min(WIN_MAX, seq)
    u, va, sga, sgb, q, k, v, k_win_t, v_win_t, b_s = _proj_prompt(
        xp, bp, seq, ng, w_in_bf, lg, lb, tabs_p, LOG2E * HEAD_DIM ** -0.5, w_keep,
        qs, ks, vs, cache_kt, cache_vt, t_new)

    b_attn = _attn(q.reshape(bp, seq, B_WIDTH), k.reshape(bp, seq, B_WIDTH),
                   v.reshape(bp, seq, B_WIDTH))
    bsp = b_spatial.reshape(A_GROUPS, CHUNK, 1)
    y_prompt = _finish(xp, u, va, sga, b_attn.reshape(bp * seq, B_WIDTH), sgb, w_spatial, bsp,
                       ga, gb, w_out_bf, fg, tm_finish).reshape(bp, seq, D_MODEL)
    k_win = k_win_t.reshape(bp, HEADS, HEAD_DIM, w_keep).transpose(0, 3, 1, 2)
    v_win = v_win_t.reshape(bp, HEADS, HEAD_DIM, w_keep).transpose(0, 3, 1, 2)

    k_new_t, v_new_t = _transpose_blocks(
        ks.reshape(bd, t_new * B_WIDTH), vs.reshape(bd, t_new * B_WIDTH), (t_new,),
        (bd, B_WIDTH), lambda t: (0, t), (t_new, B_WIDTH, bd), lambda t: (t, 0, 0))
    k_new = k_new_t.reshape(t_new, HEADS, HEAD_DIM, bd).transpose(3, 0, 1, 2)
    v_new = v_new_t.reshape(t_new, HEADS, HEAD_DIM, bd).transpose(3, 0, 1, 2)
    seqs_per_chunk = CHUNK // t_new
    eye = jnp.eye(seqs_per_chunk, dtype=w_spatial.dtype)
    wsp_s = jax.vmap(lambda w: jnp.kron(eye, w))(w_spatial[:, :t_new, :t_new])
    bsp_s = jnp.tile(b_spatial[:, :t_new], (1, seqs_per_chunk)).reshape(A_GROUPS, CHUNK, 1)
    y_sample = _finish(xs, us, vas, sgas, b_s, sgbs, wsp_s, bsp_s, ga, gb, w_out_bf, fg,
                       tm_finish).reshape(bd, t_new, D_MODEL)

    return (y_prompt, y_sample, k_win, v_win, k_new, v_new, vas.reshape(bd, t_new, A_WIDTH))
```

```python
import functools

import numpy as np
import jax
import jax.numpy as jnp
from jax import lax
from jax.experimental import pallas as pl
from jax.experimental.pallas import tpu as pltpu

D_MODEL = 1024
A_WIDTH = 512
B_WIDTH = 512
A_GROUPS = 4
CHUNK = 128
HEADS = 8
HEAD_DIM = 64
ROT_DIM = 16
ROPE_THETA = 500000.0
CONFIGS = ((128, 1), (512, 4), (2048, 16))
BAND = 128
WIN_MAX = 2048
EPS = 1e-6
LOG2E = 1.4426950408889634
IN_COLS = 3 * A_WIDTH + 4 * B_WIDTH

LANES = 128
HEADS_PER_SLAB = LANES // HEAD_DIM
N_SLABS = B_WIDTH // LANES
VMEM_LIMIT = 56 * 1024 * 1024

F32 = jnp.float32
BF16 = jnp.bfloat16


def _rms(x, g):
    return x * lax.rsqrt(jnp.mean(x * x, axis=-1, keepdims=True) + EPS) * g


def _silu(x):
    return x / (1.0 + jnp.exp(-x))


def _project_tile(x_ref, ng_ref, w_ref, lng_ref, lnb_ref, c_ref, sn_ref, sp_ref, wsp_ref, bsp_ref,
                  gna_ref, va_ref, ha_ref, sgb_ref, q_ref, k_ref, v_ref, q_scale):
    h = _rms(x_ref[...], ng_ref[...]).astype(BF16)
    nc = x_ref.shape[0] // CHUNK

    def col(g):
        return jnp.dot(h, w_ref[:, g * A_WIDTH:(g + 1) * A_WIDTH], preferred_element_type=F32)

    def rope(z, scale):
        c, sn, sp = c_ref[...], sn_ref[...], sp_ref[...]
        outs = []
        for s in range(N_SLABS):
            zs = z[:, s * LANES:(s + 1) * LANES]
            up = pltpu.roll(zs, LANES - ROT_DIM // 2, 1)
            dn = pltpu.roll(zs, ROT_DIM // 2, 1)
            r = zs * c + up * sn + dn * sp
            outs.append(r * scale if scale != 1.0 else r)
        return jnp.concatenate(outs, axis=1)

    va = jax.nn.gelu(col(1))
    mu = jnp.mean(va, axis=-1, keepdims=True)
    vc = va - mu
    va = vc * lax.rsqrt(jnp.mean(vc * vc, axis=-1, keepdims=True) + EPS) * lng_ref[...] + lnb_ref[...]
    if va_ref is not None:
        va_ref[...] = va
    va = va.astype(BF16)
    u = jax.nn.gelu(col(0))
    ri = lax.broadcasted_iota(jnp.int32, (CHUNK, CHUNK), 0)
    ci = lax.broadcasted_iota(jnp.int32, (CHUNK, CHUNK), 1)
    a_cols = []
    for g in range(A_GROUPS):
        wm = jnp.where(ri >= ci, wsp_ref[g], 0.0).astype(BF16)
        lanes = slice(g * CHUNK, (g + 1) * CHUNK)
        rhs = jnp.concatenate([va[c * CHUNK:(c + 1) * CHUNK, lanes] for c in range(nc)], axis=1)
        s = jnp.dot(wm, rhs, preferred_element_type=F32) + bsp_ref[g]
        s = jnp.concatenate([s[:, c * CHUNK:(c + 1) * CHUNK] for c in range(nc)], axis=0)
        a_cols.append(u[:, lanes] * s)
    a = jnp.concatenate(a_cols, axis=1)
    ha_ref[...] = _rms(a * _silu(col(2)), gna_ref[...]).astype(ha_ref.dtype)
    sgb_ref[...] = _silu(col(6)).astype(sgb_ref.dtype)
    q_ref[...] = rope(col(3), q_scale).astype(q_ref.dtype)
    k = rope(col(4), 1.0)
    v = col(5)
    k_ref[...] = k.astype(k_ref.dtype)
    v_ref[...] = v.astype(v_ref.dtype)
    return k, v


def _proj_sample_kernel(x_ref, ng_ref, w_ref, lng_ref, lnb_ref, c_ref, sn_ref, sp_ref, wsp_ref, bsp_ref,
                        gna_ref, va_ref, ha_ref, sgb_ref, q_ref, k_ref, v_ref, *, q_scale):
    _project_tile(x_ref, ng_ref, w_ref, lng_ref, lnb_ref, c_ref, sn_ref, sp_ref, wsp_ref, bsp_ref,
                  gna_ref, va_ref, ha_ref, sgb_ref, q_ref, k_ref, v_ref, q_scale)


def _proj_prompt_kernel(x_ref, ng_ref, w_ref, lng_ref, lnb_ref, c_ref, sn_ref, sp_ref, wsp_ref, bsp_ref,
                        gna_ref, qs_ref, kn_ref, vn_ref, ktc_ref, vtc_ref, cnt_old_ref, cnt_new_ref,
                        ha_ref, sgb_ref, q_ref, k_ref, v_ref, kt_ref, vt_ref, bs_ref, *, q_scale):
    _sample_attention(qs_ref, kn_ref, vn_ref, ktc_ref, vtc_ref, cnt_old_ref, cnt_new_ref, bs_ref)
    k, v = _project_tile(x_ref, ng_ref, w_ref, lng_ref, lnb_ref, c_ref, sn_ref, sp_ref, wsp_ref, bsp_ref,
                         gna_ref, None, ha_ref, sgb_ref, q_ref, k_ref, v_ref, q_scale)
    kt_ref[0] = k.T
    vt_ref[0] = v.T


def _proj_in_specs(tm, row, tab, fixed):
    return [
        pl.BlockSpec((tm, D_MODEL), row),
        pl.BlockSpec((1, D_MODEL), fixed),
        pl.BlockSpec((D_MODEL, IN_COLS), fixed),
        pl.BlockSpec((1, A_WIDTH), fixed),
        pl.BlockSpec((1, A_WIDTH), fixed),
        pl.BlockSpec((tm, LANES), tab),
        pl.BlockSpec((tm, LANES), tab),
        pl.BlockSpec((tm, LANES), tab),
        pl.BlockSpec((A_GROUPS, CHUNK, CHUNK), lambda *_: (0, 0, 0)),
        pl.BlockSpec((A_GROUPS, CHUNK, 1), lambda *_: (0, 0, 0)),
        pl.BlockSpec((1, A_WIDTH), fixed),
    ]


def _proj_sample(x, norm_g, w_bf, ln_g, ln_b, tabs, mix, tm, q_scale):
    n = x.shape[0]
    row = lambda i: (i, 0)
    fixed = lambda i: (0, 0)
    half = lambda dt: jax.ShapeDtypeStruct((n, A_WIDTH), dt)
    return pl.pallas_call(
        functools.partial(_proj_sample_kernel, q_scale=q_scale),
        grid=(n // tm,),
        in_specs=_proj_in_specs(tm, row, row, fixed),
        out_specs=[pl.BlockSpec((tm, A_WIDTH), row)] * 6,
        out_shape=[half(F32), half(BF16), half(BF16), half(F32), half(F32), half(F32)],
        compiler_params=pltpu.CompilerParams(
            dimension_semantics=("arbitrary",), vmem_limit_bytes=VMEM_LIMIT),
        name="proj_sample",
    )(x, norm_g, w_bf, ln_g, ln_b, *tabs, *mix)


def _proj_prompt(x, bn, seq, norm_g, w_bf, ln_g, ln_b, tabs, mix, q_scale, keep,
                 qs, k_new, v_new, cache_kt, cache_vt, t_new):
    n = bn * seq
    bd, _, w_buf = cache_kt.shape
    assert n % bd == 0, "one sample sequence per prompt tile"
    tm = n // bd
    assert tm % CHUNK == 0 and seq % tm == 0 and keep % tm == 0
    per_seq = seq // tm
    first = (seq - keep) // tm
    cnt_old, cnt_new = (jnp.asarray(c) for c in _sample_counts(w_buf, t_new))
    row = lambda b, j: (b * per_seq + j, 0)
    tab = lambda b, j: (j, 0)
    fixed = lambda b, j: (0, 0)
    new_spec = pl.BlockSpec((t_new, B_WIDTH), row)
    buf_spec = pl.BlockSpec((1, B_WIDTH, w_buf), lambda b, j: (b * per_seq + j, 0, 0))
    win_spec = pl.BlockSpec((1, B_WIDTH, tm), lambda b, j: (b, 0, jnp.maximum(j - first, 0)))
    half = lambda dt: jax.ShapeDtypeStruct((n, A_WIDTH), dt)
    win = jax.ShapeDtypeStruct((bn, B_WIDTH, keep), F32)
    return pl.pallas_call(
        functools.partial(_proj_prompt_kernel, q_scale=q_scale),
        grid=(bn, per_seq),
        in_specs=_proj_in_specs(tm, row, tab, fixed) + [
            new_spec, new_spec, new_spec, buf_spec, buf_spec,
            pl.BlockSpec(cnt_old.shape, fixed), pl.BlockSpec(cnt_new.shape, fixed)],
        out_specs=[pl.BlockSpec((tm, A_WIDTH), row)] * 5 + [win_spec, win_spec, new_spec],
        out_shape=[half(BF16)] * 5 + [win, win, jax.ShapeDtypeStruct((bd * t_new, B_WIDTH), BF16)],
        compiler_params=pltpu.CompilerParams(
            dimension_semantics=("arbitrary", "arbitrary"), vmem_limit_bytes=VMEM_LIMIT),
        name="proj_prompt",
    )(x, norm_g, w_bf, ln_g, ln_b, *tabs, *mix, qs, k_new, v_new, cache_kt, cache_vt, cnt_old, cnt_new)


def _rope_tables(pos):
    inv = ROPE_THETA ** (-jnp.arange(0, ROT_DIM, 2, dtype=F32) / ROT_DIM)
    ang = pos.astype(F32)[:, None] * inv[None, :]
    cos, sin = jnp.cos(ang), jnp.sin(ang)
    half = ROT_DIM // 2
    ones = jnp.ones((pos.shape[0], HEAD_DIM - ROT_DIM), F32)
    zeros = jnp.zeros((pos.shape[0], HEAD_DIM - half), F32)
    c = jnp.concatenate([cos, cos, ones], axis=1)
    sn = jnp.concatenate([-sin, zeros], axis=1)
    sp = jnp.concatenate([zeros[:, :half], sin, zeros[:, :HEAD_DIM - ROT_DIM]], axis=1)
    rep = lambda t: jnp.tile(t, (1, HEADS_PER_SLAB))
    return rep(c), rep(sn), rep(sp)


ATT_TILE = WIN_MAX
UNIT_UNROLL = 16
WIDEN_ROWS = 512


def _attn_kernel(qb_ref, kb_ref, vb_ref, o_ref, q_ref, k_ref, v_ref, osc, lsc, bias_ref):
    seq = qb_ref.shape[1]

    def widen(i, carry):
        sl = pl.ds(pl.multiple_of(i * WIDEN_ROWS, WIDEN_ROWS), WIDEN_ROWS)
        q_ref[sl, :] = qb_ref[0, sl, :].astype(F32)
        k_ref[sl, :] = kb_ref[0, sl, :].astype(F32)
        v_ref[sl, :] = vb_ref[0, sl, :].astype(F32)
        return carry

    lax.fori_loop(0, seq // WIDEN_ROWS, widen, 0)
    head0 = lax.broadcasted_iota(jnp.int32, (BAND, LANES), 1) < HEAD_DIM
    ii = lax.broadcasted_iota(jnp.int32, (2 * BAND, 2 * BAND), 0) % BAND
    jj = lax.broadcasted_iota(jnp.int32, (2 * BAND, 2 * BAND), 1)
    in_window = (jj <= ii + BAND) & (jj >= ii)
    bias_ref[1] = jnp.where(in_window, 0.0, -jnp.inf)
    bias_ref[0] = jnp.where(in_window & (jj >= BAND), 0.0, -jnp.inf)
    ones = jnp.ones((2 * BAND, LANES), BF16)

    def unit(c, dil, cls, m0, t):
        mp = jnp.maximum(m0 - BAND, 0)

        def two_blocks(ref):
            prev = ref[pl.ds(cls + dil * mp, BAND, stride=dil), :]
            cur = ref[pl.ds(cls + dil * m0, BAND, stride=dil), :]
            return jnp.concatenate([prev, cur], axis=0).astype(BF16)

        q = q_ref[pl.ds(cls + dil * m0, BAND, stride=dil), :]
        kk = two_blocks(k_ref)
        vv1 = jnp.concatenate([two_blocks(v_ref), ones], axis=1)
        zero = jnp.zeros_like(q)
        qs = jnp.concatenate([jnp.where(head0, q, zero), jnp.where(head0, zero, q)], axis=0)
        s = lax.dot_general(qs.astype(BF16), kk, (((1,), (1,)), ((), ())),
                            preferred_element_type=F32)
        s = s + bias_ref[jnp.where(m0 > 0, 1, 0)]
        m = jnp.max(s, axis=1, keepdims=True)
        p = jnp.exp2(s - m)
        pv = jnp.dot(p.astype(BF16), vv1, preferred_element_type=F32)
        o = jnp.where(head0, pv[:BAND, :LANES], pv[BAND:, :LANES])
        l = jnp.where(head0, pv[:BAND, LANES:], pv[BAND:, LANES:])
        mm = jnp.where(head0, jnp.broadcast_to(m[:BAND], (BAND, LANES)),
                       jnp.broadcast_to(m[BAND:], (BAND, LANES)))
        row0 = cls + dil * m0 - t * ATT_TILE
        osc[c, pl.ds(row0, BAND, stride=dil), :] = o / l
        lsc[c, pl.ds(row0, BAND, stride=dil), :] = mm + jnp.log2(l)

    units = ATT_TILE // BAND

    def tile_body(t, carry):
        for c, (_, dil) in enumerate(CONFIGS):
            blocks = units // dil

            def body(u, carry2, c=c, dil=dil, blocks=blocks):
                cls = u // blocks
                nb = u - cls * blocks
                unit(c, dil, cls, t * (ATT_TILE // dil) + nb * BAND, t)
                return carry2

            lax.fori_loop(0, units, body, 0, unroll=UNIT_UNROLL)

        rows_per_step = 256
        for r in range(ATT_TILE // rows_per_step):
            sl = pl.ds(r * rows_per_step, rows_per_step)
            ls = [lsc[c, sl, :] for c in range(len(CONFIGS))]
            mx = jnp.maximum(jnp.maximum(ls[0], ls[1]), ls[2])
            ws = [jnp.exp2(l - mx) for l in ls]
            num = ws[0] * osc[0, sl, :] + ws[1] * osc[1, sl, :] + ws[2] * osc[2, sl, :]
            dst = pl.ds(pl.multiple_of(t * ATT_TILE + r * rows_per_step, rows_per_step), rows_per_step)
            o_ref[0, dst, :] = (num / (ws[0] + ws[1] + ws[2])).astype(o_ref.dtype)
        return carry

    lax.fori_loop(0, seq // ATT_TILE, tile_body, 0)


def _attn(q, k, v):
    bn, seq, _ = q.shape
    spec = pl.BlockSpec((1, seq, LANES), lambda b, s: (b, 0, s))
    return pl.pallas_call(
        _attn_kernel,
        grid=(bn, N_SLABS),
        in_specs=[spec, spec, spec],
        out_specs=spec,
        out_shape=jax.ShapeDtypeStruct((bn, seq, B_WIDTH), BF16),
        scratch_shapes=[pltpu.VMEM((seq, LANES), F32)] * 3 + [
            pltpu.VMEM((len(CONFIGS), ATT_TILE, LANES), F32),
            pltpu.VMEM((len(CONFIGS), ATT_TILE, LANES), F32),
            pltpu.VMEM((2, 2 * BAND, 2 * BAND), F32)],
        compiler_params=pltpu.CompilerParams(
            dimension_semantics=("arbitrary", "arbitrary"), vmem_limit_bytes=VMEM_LIMIT),
        name="attn",
    )(q, k, v)


NEW_PAD = 128


def _sample_counts(w_buf, t_new):
    tq = np.arange(t_new)[:, None]

    def count(idx, exists):
        dist = w_buf + tq - idx[None, :]
        cnt = np.zeros(dist.shape, np.float32)
        for win, dil in CONFIGS:
            cnt += ((dist >= 0) & (dist <= win) & (dist % dil == 0) & exists[None, :])
        return np.tile(cnt, (HEADS, 1))

    old = np.arange(w_buf)
    new = np.arange(w_buf, w_buf + NEW_PAD)
    return count(old, old >= 0), count(new, new < w_buf + t_new)


def _sample_attention(q_ref, kn_ref, vn_ref, kt_ref, vt_ref, cnt_old_ref, cnt_new_ref, o_ref):
    t_new = q_ref.shape[0]
    rows = HEADS * t_new
    qrep = jnp.concatenate([q_ref[...]] * HEADS, axis=0)
    rh = lax.broadcasted_iota(jnp.int32, (rows, B_WIDTH), 0) // t_new
    lh = lax.broadcasted_iota(jnp.int32, (rows, B_WIDTH), 1) // HEAD_DIM
    own = rh == lh
    qbd = jnp.where(own, qrep, 0.0).astype(BF16)
    pad = jnp.zeros((NEW_PAD - t_new, B_WIDTH), F32)
    kn = jnp.concatenate([kn_ref[...], pad], axis=0).astype(BF16)
    vn = jnp.concatenate([vn_ref[...], pad], axis=0).astype(BF16)
    cnt_old, cnt_new = cnt_old_ref[...], cnt_new_ref[...]
    s_old = jnp.dot(qbd, kt_ref[0].astype(BF16), preferred_element_type=F32)
    s_new = lax.dot_general(qbd, kn, (((1,), (1,)), ((), ())), preferred_element_type=F32)
    s_old = jnp.where(cnt_old > 0.0, s_old, -jnp.inf)
    s_new = jnp.where(cnt_new > 0.0, s_new, -jnp.inf)
    m = jnp.maximum(jnp.max(s_old, axis=1, keepdims=True), jnp.max(s_new, axis=1, keepdims=True))
    p_old = jnp.exp(s_old - m) * cnt_old
    p_new = jnp.exp(s_new - m) * cnt_new
    l = jnp.sum(p_old, axis=1, keepdims=True) + jnp.sum(p_new, axis=1, keepdims=True)
    o = lax.dot_general(p_old.astype(BF16), vt_ref[0].astype(BF16), (((1,), (1,)), ((), ())),
                        preferred_element_type=F32)
    o = (o + jnp.dot(p_new.astype(BF16), vn, preferred_element_type=F32)) / l
    o = jnp.where(own, o, 0.0)
    acc = o[0:t_new]
    for h in range(1, HEADS):
        acc = acc + o[h * t_new:(h + 1) * t_new]
    o_ref[...] = acc.astype(o_ref.dtype)


def _transpose_kernel(k_ref, v_ref, kt_ref, vt_ref):
    kt_ref[0] = k_ref[...].T
    vt_ref[0] = v_ref[...].T


def _transpose_blocks(k, v, grid, src_block, src_map, out_dims, dst_map):
    src = pl.BlockSpec(src_block, src_map)
    dst = pl.BlockSpec((1, src_block[1], src_block[0]), dst_map)
    shape = jax.ShapeDtypeStruct(out_dims, F32)
    return pl.pallas_call(
        _transpose_kernel, grid=grid, in_specs=[src, src], out_specs=[dst, dst],
        out_shape=[shape, shape],
        compiler_params=pltpu.CompilerParams(dimension_semantics=("arbitrary",) * len(grid)),
        name="transpose",
    )(k, v)


def _finish_kernel(x_ref, ha_ref, b_ref, sgb_ref, gnb_ref, wout_ref, fg_ref, y_ref):
    hb = _rms(b_ref[...].astype(F32) * sgb_ref[...].astype(F32), gnb_ref[...])
    hcat = jnp.concatenate([ha_ref[...], hb.astype(BF16)], axis=1)
    y = x_ref[...] + jnp.dot(hcat, wout_ref[...], preferred_element_type=F32)
    y_ref[...] = _rms(y, fg_ref[...])


def _finish(x, ha, b, sgb, gn_b, wout_bf, final_g, tm):
    n = x.shape[0]
    row = lambda i: (i, 0)
    fixed = lambda i: (0, 0)
    half = pl.BlockSpec((tm, A_WIDTH), row)
    return pl.pallas_call(
        _finish_kernel,
        grid=(n // tm,),
        in_specs=[
            pl.BlockSpec((tm, D_MODEL), row), half, half, half,
            pl.BlockSpec((1, B_WIDTH), fixed),
            pl.BlockSpec((D_MODEL, D_MODEL), fixed),
            pl.BlockSpec((1, D_MODEL), fixed),
        ],
        out_specs=pl.BlockSpec((tm, D_MODEL), row),
        out_shape=jax.ShapeDtypeStruct((n, D_MODEL), F32),
        compiler_params=pltpu.CompilerParams(
            dimension_semantics=("arbitrary",), vmem_limit_bytes=VMEM_LIMIT),
        name="finish",
    )(x, ha, b, sgb, gn_b, wout_bf, final_g)


def kernel(x_prompt, x_sample, cache_k_win, cache_v_win, norm_g, w_in, ln_v_g, ln_v_b,
           w_spatial, b_spatial, gn_a, gn_b, w_out, final_g):
    bp, seq, _ = x_prompt.shape
    bd, t_new, _ = x_sample.shape
    past_len = seq
    w_in_bf = w_in.astype(BF16)
    w_out_bf = w_out.astype(BF16)
    ng = norm_g.reshape(1, D_MODEL)
    lg, lb = ln_v_g.reshape(1, A_WIDTH), ln_v_b.reshape(1, A_WIDTH)
    ga, gb = gn_a.reshape(1, A_WIDTH), gn_b.reshape(1, B_WIDTH)
    fg = final_g.reshape(1, D_MODEL)
    tm_sample = 512
    tm_finish = 1024

    xs = x_sample.reshape(bd * t_new, D_MODEL)
    tabs_s = _rope_tables(past_len + jnp.tile(jnp.arange(t_new), bd))
    seqs_per_chunk = CHUNK // t_new
    eye = jnp.eye(seqs_per_chunk, dtype=w_spatial.dtype)
    wsp_s = jax.vmap(lambda w: jnp.kron(eye, w))(w_spatial[:, :t_new, :t_new])
    bsp_s = jnp.tile(b_spatial[:, :t_new], (1, seqs_per_chunk)).reshape(A_GROUPS, CHUNK, 1)
    vas, has, sgbs, qs, ks, vs = _proj_sample(xs, ng, w_in_bf, lg, lb, tabs_s, (wsp_s, bsp_s, ga),
                                              tm_sample, HEAD_DIM ** -0.5)

    w_buf = cache_k_win.shape[1]
    cache_kt = cache_k_win.transpose(0, 2, 3, 1).reshape(bd, B_WIDTH, w_buf)
    cache_vt = cache_v_win.transpose(0, 2, 3, 1).reshape(bd, B_WIDTH, w_buf)
    xp = x_prompt.reshape(bp * seq, D_MODEL)
    tabs_p = _rope_tables(jnp.arange(seq))
    w_keep = min(WIN_MAX, seq)
    bsp = b_spatial.reshape(A_GROUPS, CHUNK, 1)
    ha, sgb, q, k, v, k_win_t, v_win_t, b_s = _proj_prompt(
        xp, bp, seq, ng, w_in_bf, lg, lb, tabs_p, (w_spatial, bsp, ga), LOG2E * HEAD_DIM ** -0.5,
        w_keep, qs, ks, vs, cache_kt, cache_vt, t_new)

    b_attn = _attn(q.reshape(bp, seq, B_WIDTH), k.reshape(bp, seq, B_WIDTH),
                   v.reshape(bp, seq, B_WIDTH))
    y_prompt = _finish(xp, ha, b_attn.reshape(bp * seq, B_WIDTH), sgb, gb, w_out_bf, fg,
                       tm_finish).reshape(bp, seq, D_MODEL)
    k_win = k_win_t.reshape(bp, HEADS, HEAD_DIM, w_keep).transpose(0, 3, 1, 2)
    v_win = v_win_t.reshape(bp, HEADS, HEAD_DIM, w_keep).transpose(0, 3, 1, 2)

    k_new_t, v_new_t = _transpose_blocks(
        ks.reshape(bd, t_new * B_WIDTH), vs.reshape(bd, t_new * B_WIDTH), (t_new,),
        (bd, B_WIDTH), lambda t: (0, t), (t_new, B_WIDTH, bd), lambda t: (t, 0, 0))
    k_new = k_new_t.reshape(t_new, HEADS, HEAD_DIM, bd).transpose(3, 0, 1, 2)
    v_new = v_new_t.reshape(t_new, HEADS, HEAD_DIM, bd).transpose(3, 0, 1, 2)
    y_sample = _finish(xs, has, b_s, sgbs, gb, w_out_bf, fg, tm_finish).reshape(bd, t_new, D_MODEL)

    return (y_prompt, y_sample, k_win, v_win, k_new, v_new, vas.reshape(bd, t_new, A_WIDTH))
```

```python
import functools

import numpy as np
import jax
import jax.numpy as jnp
from jax import lax
from jax.experimental import pallas as pl
from jax.experimental.pallas import tpu as pltpu

D_MODEL = 1024
A_WIDTH = 512
B_WIDTH = 512
A_GROUPS = 4
CHUNK = 128
HEADS = 8
HEAD_DIM = 64
ROT_DIM = 16
ROPE_THETA = 500000.0
CONFIGS = ((128, 1), (512, 4), (2048, 16))
BAND = 128
WIN_MAX = 2048
EPS = 1e-6
LOG2E = 1.4426950408889634
IN_COLS = 3 * A_WIDTH + 4 * B_WIDTH

LANES = 128
HEADS_PER_SLAB = LANES // HEAD_DIM
N_SLABS = B_WIDTH // LANES
VMEM_LIMIT = 56 * 1024 * 1024

F32 = jnp.float32
BF16 = jnp.bfloat16


def _rms(x, g):
    return x * lax.rsqrt(jnp.mean(x * x, axis=-1, keepdims=True) + EPS) * g


def _silu(x):
    return x / (1.0 + jnp.exp(-x))


def _project_tile(x_ref, ng_ref, w_ref, lng_ref, lnb_ref, c_ref, sn_ref, sp_ref, wsp_ref, bsp_ref,
                  gna_ref, va_ref, ha_ref, sgb_ref, q_ref, k_ref, v_ref, q_scale, emit_kv=None):
    h = _rms(x_ref[...], ng_ref[...]).astype(BF16)
    nc = x_ref.shape[0] // CHUNK

    def col(g):
        return jnp.dot(h, w_ref[:, g * A_WIDTH:(g + 1) * A_WIDTH], preferred_element_type=F32)

    def rope(z, scale):
        c, sn, sp = c_ref[...], sn_ref[...], sp_ref[...]
        outs = []
        for s in range(N_SLABS):
            zs = z[:, s * LANES:(s + 1) * LANES]
            up = pltpu.roll(zs, LANES - ROT_DIM // 2, 1)
            dn = pltpu.roll(zs, ROT_DIM // 2, 1)
            r = zs * c + up * sn + dn * sp
            outs.append(r * scale if scale != 1.0 else r)
        return jnp.concatenate(outs, axis=1)

    va = jax.nn.gelu(col(1))
    mu = jnp.mean(va, axis=-1, keepdims=True)
    vc = va - mu
    va = vc * lax.rsqrt(jnp.mean(vc * vc, axis=-1, keepdims=True) + EPS) * lng_ref[...] + lnb_ref[...]
    if va_ref is not None:
        va_ref[...] = va
    va = va.astype(BF16)
    u = jax.nn.gelu(col(0))
    ri = lax.broadcasted_iota(jnp.int32, (CHUNK, CHUNK), 0)
    ci = lax.broadcasted_iota(jnp.int32, (CHUNK, CHUNK), 1)
    a_cols = []
    for g in range(A_GROUPS):
        wm = jnp.where(ri >= ci, wsp_ref[g], 0.0).astype(BF16)
        lanes = slice(g * CHUNK, (g + 1) * CHUNK)
        rhs = jnp.concatenate([va[c * CHUNK:(c + 1) * CHUNK, lanes] for c in range(nc)], axis=1)
        s = jnp.dot(wm, rhs, preferred_element_type=F32) + bsp_ref[g]
        s = jnp.concatenate([s[:, c * CHUNK:(c + 1) * CHUNK] for c in range(nc)], axis=0)
        a_cols.append(u[:, lanes] * s)
    a = jnp.concatenate(a_cols, axis=1)
    ha_ref[...] = _rms(a * _silu(col(2)), gna_ref[...]).astype(ha_ref.dtype)
    q_ref[...] = rope(col(3), q_scale).astype(q_ref.dtype)
    k = rope(col(4), 1.0)
    v = col(5)
    k_ref[...] = k.astype(k_ref.dtype)
    v_ref[...] = v.astype(v_ref.dtype)
    if emit_kv is not None:
        emit_kv(k, v)
    sgb_ref[...] = _silu(col(6)).astype(sgb_ref.dtype)


def _proj_sample_kernel(x_ref, ng_ref, w_ref, lng_ref, lnb_ref, c_ref, sn_ref, sp_ref, wsp_ref, bsp_ref,
                        gna_ref, va_ref, ha_ref, sgb_ref, q_ref, k_ref, v_ref, *, q_scale):
    _project_tile(x_ref, ng_ref, w_ref, lng_ref, lnb_ref, c_ref, sn_ref, sp_ref, wsp_ref, bsp_ref,
                  gna_ref, va_ref, ha_ref, sgb_ref, q_ref, k_ref, v_ref, q_scale)


def _proj_prompt_kernel(x_ref, ng_ref, w_ref, lng_ref, lnb_ref, c_ref, sn_ref, sp_ref, wsp_ref, bsp_ref,
                        gna_ref, qs_ref, kn_ref, vn_ref, ktc_ref, vtc_ref, cnt_old_ref, cnt_new_ref,
                        ha_ref, sgb_ref, q_ref, k_ref, v_ref, kt_ref, vt_ref, bs_ref, *, q_scale):
    _sample_attention(qs_ref, kn_ref, vn_ref, ktc_ref, vtc_ref, cnt_old_ref, cnt_new_ref, bs_ref)

    def emit_kv(k, v):
        kt_ref[0] = k.T
        vt_ref[0] = v.T

    _project_tile(x_ref, ng_ref, w_ref, lng_ref, lnb_ref, c_ref, sn_ref, sp_ref, wsp_ref, bsp_ref,
                  gna_ref, None, ha_ref, sgb_ref, q_ref, k_ref, v_ref, q_scale, emit_kv)


def _proj_in_specs(tm, row, tab, fixed):
    return [
        pl.BlockSpec((tm, D_MODEL), row),
        pl.BlockSpec((1, D_MODEL), fixed),
        pl.BlockSpec((D_MODEL, IN_COLS), fixed),
        pl.BlockSpec((1, A_WIDTH), fixed),
        pl.BlockSpec((1, A_WIDTH), fixed),
        pl.BlockSpec((tm, LANES), tab),
        pl.BlockSpec((tm, LANES), tab),
        pl.BlockSpec((tm, LANES), tab),
        pl.BlockSpec((A_GROUPS, CHUNK, CHUNK), lambda *_: (0, 0, 0)),
        pl.BlockSpec((A_GROUPS, CHUNK, 1), lambda *_: (0, 0, 0)),
        pl.BlockSpec((1, A_WIDTH), fixed),
    ]


def _proj_sample(x, norm_g, w_bf, ln_g, ln_b, tabs, mix, tm, q_scale):
    n = x.shape[0]
    row = lambda i: (i, 0)
    fixed = lambda i: (0, 0)
    half = lambda dt: jax.ShapeDtypeStruct((n, A_WIDTH), dt)
    return pl.pallas_call(
        functools.partial(_proj_sample_kernel, q_scale=q_scale),
        grid=(n // tm,),
        in_specs=_proj_in_specs(tm, row, row, fixed),
        out_specs=[pl.BlockSpec((tm, A_WIDTH), row)] * 6,
        out_shape=[half(F32), half(BF16), half(BF16), half(F32), half(F32), half(F32)],
        compiler_params=pltpu.CompilerParams(
            dimension_semantics=("arbitrary",), vmem_limit_bytes=VMEM_LIMIT),
        name="proj_sample",
    )(x, norm_g, w_bf, ln_g, ln_b, *tabs, *mix)


def _proj_prompt(x, bn, seq, norm_g, w_bf, ln_g, ln_b, tabs, mix, q_scale, keep,
                 qs, k_new, v_new, cache_kt, cache_vt, t_new):
    n = bn * seq
    bd, _, w_buf = cache_kt.shape
    assert n % bd == 0, "one sample sequence per prompt tile"
    tm = n // bd
    assert tm % CHUNK == 0 and seq % tm == 0 and keep % tm == 0
    per_seq = seq // tm
    first = (seq - keep) // tm
    cnt_old, cnt_new = (jnp.asarray(c) for c in _sample_counts(w_buf, t_new))
    row = lambda b, j: (b * per_seq + j, 0)
    tab = lambda b, j: (j, 0)
    fixed = lambda b, j: (0, 0)
    new_spec = pl.BlockSpec((t_new, B_WIDTH), row)
    buf_spec = pl.BlockSpec((1, B_WIDTH, w_buf), lambda b, j: (b * per_seq + j, 0, 0))
    win_spec = pl.BlockSpec((1, B_WIDTH, tm), lambda b, j: (b, 0, jnp.maximum(j - first, 0)))
    half = lambda dt: jax.ShapeDtypeStruct((n, A_WIDTH), dt)
    win = jax.ShapeDtypeStruct((bn, B_WIDTH, keep), F32)
    return pl.pallas_call(
        functools.partial(_proj_prompt_kernel, q_scale=q_scale),
        grid=(bn, per_seq),
        in_specs=_proj_in_specs(tm, row, tab, fixed) + [
            new_spec, new_spec, new_spec, buf_spec, buf_spec,
            pl.BlockSpec(cnt_old.shape, fixed), pl.BlockSpec(cnt_new.shape, fixed)],
        out_specs=[pl.BlockSpec((tm, A_WIDTH), row)] * 5 + [win_spec, win_spec, new_spec],
        out_shape=[half(BF16)] * 5 + [win, win, jax.ShapeDtypeStruct((bd * t_new, B_WIDTH), BF16)],
        compiler_params=pltpu.CompilerParams(
            dimension_semantics=("arbitrary", "arbitrary"), vmem_limit_bytes=VMEM_LIMIT),
        name="proj_prompt",
    )(x, norm_g, w_bf, ln_g, ln_b, *tabs, *mix, qs, k_new, v_new, cache_kt, cache_vt, cnt_old, cnt_new)


def _rope_tables(pos):
    inv = ROPE_THETA ** (-jnp.arange(0, ROT_DIM, 2, dtype=F32) / ROT_DIM)
    ang = pos.astype(F32)[:, None] * inv[None, :]
    cos, sin = jnp.cos(ang), jnp.sin(ang)
    half = ROT_DIM // 2
    ones = jnp.ones((pos.shape[0], HEAD_DIM - ROT_DIM), F32)
    zeros = jnp.zeros((pos.shape[0], HEAD_DIM - half), F32)
    c = jnp.concatenate([cos, cos, ones], axis=1)
    sn = jnp.concatenate([-sin, zeros], axis=1)
    sp = jnp.concatenate([zeros[:, :half], sin, zeros[:, :HEAD_DIM - ROT_DIM]], axis=1)
    rep = lambda t: jnp.tile(t, (1, HEADS_PER_SLAB))
    return rep(c), rep(sn), rep(sp)


ATT_TILE = WIN_MAX
UNIT_UNROLL = 16
WIDEN_ROWS = 512
REGROUP = 4


def _attn_kernel(qb_ref, kb_ref, vb_ref, o_ref, q4_ref, k4_ref, v4_ref, stage_ref, osc, lsc, bias_ref):
    seq = qb_ref.shape[1]
    quarter = seq // REGROUP

    def widen(i, carry):
        src = pl.ds(pl.multiple_of(i * WIDEN_ROWS, WIDEN_ROWS), WIDEN_ROWS)
        per_class = WIDEN_ROWS // REGROUP
        for n, (b_ref, w_ref) in enumerate(((qb_ref, q4_ref), (kb_ref, k4_ref), (vb_ref, v4_ref))):
            stage_ref[n] = b_ref[0, src, :].astype(F32)
            for e in range(REGROUP):
                dst = pl.ds(pl.multiple_of(e * quarter + i * per_class, per_class), per_class)
                w_ref[dst, :] = stage_ref[n, pl.ds(e, per_class, stride=REGROUP), :]
        return carry

    lax.fori_loop(0, seq // WIDEN_ROWS, widen, 0)
    head0 = lax.broadcasted_iota(jnp.int32, (BAND, LANES), 1) < HEAD_DIM
    ii = lax.broadcasted_iota(jnp.int32, (2 * BAND, 2 * BAND), 0) % BAND
    jj = lax.broadcasted_iota(jnp.int32, (2 * BAND, 2 * BAND), 1)
    in_window = (jj <= ii + BAND) & (jj >= ii)
    bias_ref[1] = jnp.where(in_window, 0.0, -jnp.inf)
    bias_ref[0] = jnp.where(in_window & (jj >= BAND), 0.0, -jnp.inf)
    ones = jnp.ones((2 * BAND, LANES), BF16)

    def unit(c, dil, cls, m0, t):
        mp = jnp.maximum(m0 - BAND, 0)

        def block(b_ref, w_ref, start):
            if dil == 1:
                return b_ref[0, pl.ds(pl.multiple_of(start, BAND), BAND), :]
            if dil == REGROUP:
                rows = pl.ds(pl.multiple_of(cls * quarter + start, BAND), BAND)
            else:
                sub = cls // REGROUP
                step = dil // REGROUP
                rows = pl.ds((cls - sub * REGROUP) * quarter + sub + step * start, BAND, stride=step)
            return w_ref[rows, :].astype(BF16)

        q = block(qb_ref, q4_ref, m0)
        kk = jnp.concatenate([block(kb_ref, k4_ref, mp), block(kb_ref, k4_ref, m0)], axis=0)
        vv = jnp.concatenate([block(vb_ref, v4_ref, mp), block(vb_ref, v4_ref, m0)], axis=0)
        vv1 = jnp.concatenate([vv, ones], axis=1)
        zero = jnp.zeros_like(q)
        qs = jnp.concatenate([jnp.where(head0, q, zero), jnp.where(head0, zero, q)], axis=0)
        s = lax.dot_general(qs, kk, (((1,), (1,)), ((), ())), preferred_element_type=F32)
        s = s + bias_ref[jnp.where(m0 > 0, 1, 0)]
        m = jnp.max(s, axis=1, keepdims=True)
        p = jnp.exp2(s - m)
        pv = jnp.dot(p.astype(BF16), vv1, preferred_element_type=F32)
        o = jnp.where(head0, pv[:BAND, :LANES], pv[BAND:, :LANES])
        l = jnp.where(head0, pv[:BAND, LANES:], pv[BAND:, LANES:])
        mm = jnp.where(head0, jnp.broadcast_to(m[:BAND], (BAND, LANES)),
                       jnp.broadcast_to(m[BAND:], (BAND, LANES)))
        row0 = cls + dil * m0 - t * ATT_TILE
        osc[c, pl.ds(row0, BAND, stride=dil), :] = o / l
        lsc[c, pl.ds(row0, BAND, stride=dil), :] = mm + jnp.log2(l)

    units = ATT_TILE // BAND

    def tile_body(t, carry):
        for c, (_, dil) in enumerate(CONFIGS):
            blocks = units // dil

            def body(u, carry2, c=c, dil=dil, blocks=blocks):
                cls = u // blocks
                nb = u - cls * blocks
                unit(c, dil, cls, t * (ATT_TILE // dil) + nb * BAND, t)
                return carry2

            lax.fori_loop(0, units, body, 0, unroll=UNIT_UNROLL)

        rows_per_step = 256
        for r in range(ATT_TILE // rows_per_step):
            sl = pl.ds(r * rows_per_step, rows_per_step)
            ls = [lsc[c, sl, :] for c in range(len(CONFIGS))]
            mx = jnp.maximum(jnp.maximum(ls[0], ls[1]), ls[2])
            ws = [jnp.exp2(l - mx) for l in ls]
            num = ws[0] * osc[0, sl, :] + ws[1] * osc[1, sl, :] + ws[2] * osc[2, sl, :]
            dst = pl.ds(pl.multiple_of(t * ATT_TILE + r * rows_per_step, rows_per_step), rows_per_step)
            o_ref[0, dst, :] = (num / (ws[0] + ws[1] + ws[2])).astype(o_ref.dtype)
        return carry

    lax.fori_loop(0, seq // ATT_TILE, tile_body, 0)


def _attn(q, k, v):
    bn, seq, _ = q.shape
    spec = pl.BlockSpec((1, seq, LANES), lambda b, s: (b, 0, s))
    return pl.pallas_call(
        _attn_kernel,
        grid=(bn, N_SLABS),
        in_specs=[spec, spec, spec],
        out_specs=spec,
        out_shape=jax.ShapeDtypeStruct((bn, seq, B_WIDTH), BF16),
        scratch_shapes=[pltpu.VMEM((seq, LANES), F32)] * 3 + [
            pltpu.VMEM((3, WIDEN_ROWS, LANES), F32),
            pltpu.VMEM((len(CONFIGS), ATT_TILE, LANES), F32),
            pltpu.VMEM((len(CONFIGS), ATT_TILE, LANES), F32),
            pltpu.VMEM((2, 2 * BAND, 2 * BAND), F32)],
        compiler_params=pltpu.CompilerParams(
            dimension_semantics=("arbitrary", "arbitrary"), vmem_limit_bytes=VMEM_LIMIT),
        name="attn",
    )(q, k, v)


NEW_PAD = 128


def _sample_counts(w_buf, t_new):
    tq = np.arange(t_new)[:, None]

    def count(idx, exists):
        dist = w_buf + tq - idx[None, :]
        cnt = np.zeros(dist.shape, np.float32)
        for win, dil in CONFIGS:
            cnt += ((dist >= 0) & (dist <= win) & (dist % dil == 0) & exists[None, :])
        return np.tile(cnt, (HEADS, 1))

    old = np.arange(w_buf)
    new = np.arange(w_buf, w_buf + NEW_PAD)
    return count(old, old >= 0), count(new, new < w_buf + t_new)


def _sample_attention(q_ref, kn_ref, vn_ref, kt_ref, vt_ref, cnt_old_ref, cnt_new_ref, o_ref):
    t_new = q_ref.shape[0]
    rows = HEADS * t_new
    qrep = jnp.concatenate([q_ref[...]] * HEADS, axis=0)
    rh = lax.broadcasted_iota(jnp.int32, (rows, B_WIDTH), 0) // t_new
    lh = lax.broadcasted_iota(jnp.int32, (rows, B_WIDTH), 1) // HEAD_DIM
    own = rh == lh
    qbd = jnp.where(own, qrep, 0.0).astype(BF16)
    pad = jnp.zeros((NEW_PAD - t_new, B_WIDTH), F32)
    kn = jnp.concatenate([kn_ref[...], pad], axis=0).astype(BF16)
    vn = jnp.concatenate([vn_ref[...], pad], axis=0).astype(BF16)
    cnt_old, cnt_new = cnt_old_ref[...], cnt_new_ref[...]
    s_old = jnp.dot(qbd, kt_ref[0].astype(BF16), preferred_element_type=F32)
    s_new = lax.dot_general(qbd, kn, (((1,), (1,)), ((), ())), preferred_element_type=F32)
    s_old = jnp.where(cnt_old > 0.0, s_old, -jnp.inf)
    s_new = jnp.where(cnt_new > 0.0, s_new, -jnp.inf)
    m = jnp.maximum(jnp.max(s_old, axis=1, keepdims=True), jnp.max(s_new, axis=1, keepdims=True))
    p_old = jnp.exp(s_old - m) * cnt_old
    p_new = jnp.exp(s_new - m) * cnt_new
    l = jnp.sum(p_old, axis=1, keepdims=True) + jnp.sum(p_new, axis=1, keepdims=True)
    o = lax.dot_general(p_old.astype(BF16), vt_ref[0].astype(BF16), (((1,), (1,)), ((), ())),
                        preferred_element_type=F32)
    o = (o + jnp.dot(p_new.astype(BF16), vn, preferred_element_type=F32)) / l
    o = jnp.where(own, o, 0.0)
    acc = o[0:t_new]
    for h in range(1, HEADS):
        acc = acc + o[h * t_new:(h + 1) * t_new]
    o_ref[...] = acc.astype(o_ref.dtype)


def _transpose_kernel(k_ref, v_ref, kt_ref, vt_ref):
    kt_ref[0] = k_ref[...].T
    vt_ref[0] = v_ref[...].T


def _transpose_blocks(k, v, grid, src_block, src_map, out_dims, dst_map):
    src = pl.BlockSpec(src_block, src_map)
    dst = pl.BlockSpec((1, src_block[1], src_block[0]), dst_map)
    shape = jax.ShapeDtypeStruct(out_dims, F32)
    return pl.pallas_call(
        _transpose_kernel, grid=grid, in_specs=[src, src], out_specs=[dst, dst],
        out_shape=[shape, shape],
        compiler_params=pltpu.CompilerParams(dimension_semantics=("arbitrary",) * len(grid)),
        name="transpose",
    )(k, v)


def _finish_kernel(x_ref, ha_ref, b_ref, sgb_ref, gnb_ref, wout_ref, fg_ref, y_ref):
    hb = _rms(b_ref[...].astype(F32) * sgb_ref[...].astype(F32), gnb_ref[...])
    hcat = jnp.concatenate([ha_ref[...], hb.astype(BF16)], axis=1)
    y = x_ref[...] + jnp.dot(hcat, wout_ref[...], preferred_element_type=F32)
    y_ref[...] = _rms(y, fg_ref[...])


def _finish(x, ha, b, sgb, gn_b, wout_bf, final_g, tm):
    n = x.shape[0]
    row = lambda i: (i, 0)
    fixed = lambda i: (0, 0)
    half = pl.BlockSpec((tm, A_WIDTH), row)
    return pl.pallas_call(
        _finish_kernel,
        grid=(n // tm,),
        in_specs=[
            pl.BlockSpec((tm, D_MODEL), row), half, half, half,
            pl.BlockSpec((1, B_WIDTH), fixed),
            pl.BlockSpec((D_MODEL, D_MODEL), fixed),
            pl.BlockSpec((1, D_MODEL), fixed),
        ],
        out_specs=pl.BlockSpec((tm, D_MODEL), row),
        out_shape=jax.ShapeDtypeStruct((n, D_MODEL), F32),
        compiler_params=pltpu.CompilerParams(
            dimension_semantics=("arbitrary",), vmem_limit_bytes=VMEM_LIMIT),
        name="finish",
    )(x, ha, b, sgb, gn_b, wout_bf, final_g)


def kernel(x_prompt, x_sample, cache_k_win, cache_v_win, norm_g, w_in, ln_v_g, ln_v_b,
           w_spatial, b_spatial, gn_a, gn_b, w_out, final_g):
    bp, seq, _ = x_prompt.shape
    bd, t_new, _ = x_sample.shape
    past_len = seq
    w_in_bf = w_in.astype(BF16)
    w_out_bf = w_out.astype(BF16)
    ng = norm_g.reshape(1, D_MODEL)
    lg, lb = ln_v_g.reshape(1, A_WIDTH), ln_v_b.reshape(1, A_WIDTH)
    ga, gb = gn_a.reshape(1, A_WIDTH), gn_b.reshape(1, B_WIDTH)
    fg = final_g.reshape(1, D_MODEL)
    tm_sample = 512
    tm_finish = 1024

    xs = x_sample.reshape(bd * t_new, D_MODEL)
    tabs_s = _rope_tables(past_len + jnp.tile(jnp.arange(t_new), bd))
    seqs_per_chunk = CHUNK // t_new
    eye = jnp.eye(seqs_per_chunk, dtype=w_spatial.dtype)
    wsp_s = jax.vmap(lambda w: jnp.kron(eye, w))(w_spatial[:, :t_new, :t_new])
    bsp_s = jnp.tile(b_spatial[:, :t_new], (1, seqs_per_chunk)).reshape(A_GROUPS, CHUNK, 1)
    vas, has, sgbs, qs, ks, vs = _proj_sample(xs, ng, w_in_bf, lg, lb, tabs_s, (wsp_s, bsp_s, ga),
                                              tm_sample, HEAD_DIM ** -0.5)

    w_buf = cache_k_win.shape[1]
    cache_kt = cache_k_win.transpose(0, 2, 3, 1).reshape(bd, B_WIDTH, w_buf)
    cache_vt = cache_v_win.transpose(0, 2, 3, 1).reshape(bd, B_WIDTH, w_buf)
    xp = x_prompt.reshape(bp * seq, D_MODEL)
    tabs_p = _rope_tables(jnp.arange(seq))
    w_keep = min(WIN_MAX, seq)
    bsp = b_spatial.reshape(A_GROUPS, CHUNK, 1)
    ha, sgb, q, k, v, k_win_t, v_win_t, b_s = _proj_prompt(
        xp, bp, seq, ng, w_in_bf, lg, lb, tabs_p, (w_spatial, bsp, ga), LOG2E * HEAD_DIM ** -0.5,
        w_keep, qs, ks, vs, cache_kt, cache_vt, t_new)

    b_attn = _attn(q.reshape(bp, seq, B_WIDTH), k.reshape(bp, seq, B_WIDTH),
                   v.reshape(bp, seq, B_WIDTH))
    y_prompt = _finish(xp, ha, b_attn.reshape(bp * seq, B_WIDTH), sgb, gb, w_out_bf, fg,
                       tm_finish).reshape(bp, seq, D_MODEL)
    k_win = k_win_t.reshape(bp, HEADS, HEAD_DIM, w_keep).transpose(0, 3, 1, 2)
    v_win = v_win_t.reshape(bp, HEADS, HEAD_DIM, w_keep).transpose(0, 3, 1, 2)

    k_new_t, v_new_t = _transpose_blocks(
        ks.reshape(bd, t_new * B_WIDTH), vs.reshape(bd, t_new * B_WIDTH), (t_new,),
        (bd, B_WIDTH), lambda t: (0, t), (t_new, B_WIDTH, bd), lambda t: (t, 0, 0))
    k_new = k_new_t.reshape(t_new, HEADS, HEAD_DIM, bd).transpose(3, 0, 1, 2)
    v_new = v_new_t.reshape(t_new, HEADS, HEAD_DIM, bd).transpose(3, 0, 1, 2)
    y_sample = _finish(xs, has, b_s, sgbs, gb, w_out_bf, fg, tm_finish).reshape(bd, t_new, D_MODEL)

    return (y_prompt, y_sample, k_win, v_win, k_new, v_new, vas.reshape(bd, t_new, A_WIDTH))
```

```python
import functools

import numpy as np
import jax
import jax.numpy as jnp
from jax import lax
from jax.experimental import pallas as pl
from jax.experimental.pallas import tpu as pltpu

D_MODEL = 1024
A_WIDTH = 512
B_WIDTH = 512
A_GROUPS = 4
CHUNK = 128
HEADS = 8
HEAD_DIM = 64
ROT_DIM = 16
ROPE_THETA = 500000.0
CONFIGS = ((128, 1), (512, 4), (2048, 16))
BAND = 128
WIN_MAX = 2048
EPS = 1e-6
LOG2E = 1.4426950408889634
IN_COLS = 3 * A_WIDTH + 4 * B_WIDTH

LANES = 128
HEADS_PER_SLAB = LANES // HEAD_DIM
N_SLABS = B_WIDTH // LANES
VMEM_LIMIT = 56 * 1024 * 1024

F32 = jnp.float32
BF16 = jnp.bfloat16


def _rms(x, g):
    return x * lax.rsqrt(jnp.mean(x * x, axis=-1, keepdims=True) + EPS) * g


def _silu(x):
    return x / (1.0 + jnp.exp(-x))


def _project_tile(x_ref, ng_ref, w_ref, lng_ref, lnb_ref, c_ref, sn_ref, sp_ref, wsp_ref, bsp_ref,
                  gna_ref, va_ref, ha_ref, sgb_ref, q_ref, k_ref, v_ref, q_scale, emit_kv=None):
    h = _rms(x_ref[...], ng_ref[...]).astype(BF16)
    nc = x_ref.shape[0] // CHUNK

    def col(g):
        return jnp.dot(h, w_ref[:, g * A_WIDTH:(g + 1) * A_WIDTH], preferred_element_type=F32)

    def rope(z, scale):
        c, sn, sp = c_ref[...], sn_ref[...], sp_ref[...]
        outs = []
        for s in range(N_SLABS):
            zs = z[:, s * LANES:(s + 1) * LANES]
            up = pltpu.roll(zs, LANES - ROT_DIM // 2, 1)
            dn = pltpu.roll(zs, ROT_DIM // 2, 1)
            r = zs * c + up * sn + dn * sp
            outs.append(r * scale if scale != 1.0 else r)
        return jnp.concatenate(outs, axis=1)

    va = jax.nn.gelu(col(1))
    mu = jnp.mean(va, axis=-1, keepdims=True)
    vc = va - mu
    va = vc * lax.rsqrt(jnp.mean(vc * vc, axis=-1, keepdims=True) + EPS) * lng_ref[...] + lnb_ref[...]
    if va_ref is not None:
        va_ref[...] = va
    va = va.astype(BF16)
    u = jax.nn.gelu(col(0))
    ri = lax.broadcasted_iota(jnp.int32, (CHUNK, CHUNK), 0)
    ci = lax.broadcasted_iota(jnp.int32, (CHUNK, CHUNK), 1)
    a_cols = []
    for g in range(A_GROUPS):
        wm = jnp.where(ri >= ci, wsp_ref[g], 0.0).astype(BF16)
        lanes = slice(g * CHUNK, (g + 1) * CHUNK)
        rhs = jnp.concatenate([va[c * CHUNK:(c + 1) * CHUNK, lanes] for c in range(nc)], axis=1)
        s = jnp.dot(wm, rhs, preferred_element_type=F32) + bsp_ref[g]
        s = jnp.concatenate([s[:, c * CHUNK:(c + 1) * CHUNK] for c in range(nc)], axis=0)
        a_cols.append(u[:, lanes] * s)
    a = jnp.concatenate(a_cols, axis=1)
    ha_ref[...] = _rms(a * _silu(col(2)), gna_ref[...]).astype(ha_ref.dtype)
    q_ref[...] = rope(col(3), q_scale).astype(q_ref.dtype)
    k = rope(col(4), 1.0)
    v = col(5)
    k_ref[...] = k.astype(k_ref.dtype)
    v_ref[...] = v.astype(v_ref.dtype)
    if emit_kv is not None:
        emit_kv(k, v)
    sgb_ref[...] = _silu(col(6)).astype(sgb_ref.dtype)


def _proj_sample_kernel(x_ref, ng_ref, w_ref, lng_ref, lnb_ref, c_ref, sn_ref, sp_ref, wsp_ref, bsp_ref,
                        gna_ref, va_ref, ha_ref, sgb_ref, q_ref, k_ref, v_ref, *, q_scale):
    _project_tile(x_ref, ng_ref, w_ref, lng_ref, lnb_ref, c_ref, sn_ref, sp_ref, wsp_ref, bsp_ref,
                  gna_ref, va_ref, ha_ref, sgb_ref, q_ref, k_ref, v_ref, q_scale)


def _proj_prompt_kernel(x_ref, ng_ref, w_ref, lng_ref, lnb_ref, c_ref, sn_ref, sp_ref, wsp_ref, bsp_ref,
                        gna_ref, qs_ref, kn_ref, vn_ref, ktc_ref, vtc_ref, cnt_old_ref, cnt_new_ref,
                        ha_ref, sgb_ref, q_ref, k_ref, v_ref, kt_ref, vt_ref, bs_ref, *, q_scale):
    _sample_attention(qs_ref, kn_ref, vn_ref, ktc_ref, vtc_ref, cnt_old_ref, cnt_new_ref, bs_ref)

    def emit_kv(k, v):
        kt_ref[0] = k.T
        vt_ref[0] = v.T

    _project_tile(x_ref, ng_ref, w_ref, lng_ref, lnb_ref, c_ref, sn_ref, sp_ref, wsp_ref, bsp_ref,
                  gna_ref, None, ha_ref, sgb_ref, q_ref, k_ref, v_ref, q_scale, emit_kv)


def _proj_in_specs(tm, row, tab, fixed):
    return [
        pl.BlockSpec((tm, D_MODEL), row),
        pl.BlockSpec((1, D_MODEL), fixed),
        pl.BlockSpec((D_MODEL, IN_COLS), fixed),
        pl.BlockSpec((1, A_WIDTH), fixed),
        pl.BlockSpec((1, A_WIDTH), fixed),
        pl.BlockSpec((tm, LANES), tab),
        pl.BlockSpec((tm, LANES), tab),
        pl.BlockSpec((tm, LANES), tab),
        pl.BlockSpec((A_GROUPS, CHUNK, CHUNK), lambda *_: (0, 0, 0)),
        pl.BlockSpec((A_GROUPS, CHUNK, 1), lambda *_: (0, 0, 0)),
        pl.BlockSpec((1, A_WIDTH), fixed),
    ]


def _proj_sample(x, norm_g, w_bf, ln_g, ln_b, tabs, mix, tm, q_scale):
    n = x.shape[0]
    row = lambda i: (i, 0)
    fixed = lambda i: (0, 0)
    half = lambda dt: jax.ShapeDtypeStruct((n, A_WIDTH), dt)
    return pl.pallas_call(
        functools.partial(_proj_sample_kernel, q_scale=q_scale),
        grid=(n // tm,),
        in_specs=_proj_in_specs(tm, row, row, fixed),
        out_specs=[pl.BlockSpec((tm, A_WIDTH), row)] * 6,
        out_shape=[half(F32), half(BF16), half(BF16), half(F32), half(F32), half(F32)],
        compiler_params=pltpu.CompilerParams(
            dimension_semantics=("arbitrary",), vmem_limit_bytes=VMEM_LIMIT),
        name="proj_sample",
    )(x, norm_g, w_bf, ln_g, ln_b, *tabs, *mix)


def _proj_prompt(x, bn, seq, norm_g, w_bf, ln_g, ln_b, tabs, mix, q_scale, keep,
                 qs, k_new, v_new, cache_kt, cache_vt, t_new):
    n = bn * seq
    bd, _, w_buf = cache_kt.shape
    assert n % bd == 0, "one sample sequence per prompt tile"
    tm = n // bd
    assert tm % CHUNK == 0 and seq % tm == 0 and keep % tm == 0
    per_seq = seq // tm
    first = (seq - keep) // tm
    cnt_old, cnt_new = (jnp.asarray(c) for c in _sample_counts(w_buf, t_new))
    row = lambda b, j: (b * per_seq + j, 0)
    tab = lambda b, j: (j, 0)
    fixed = lambda b, j: (0, 0)
    new_spec = pl.BlockSpec((t_new, B_WIDTH), row)
    buf_spec = pl.BlockSpec((1, B_WIDTH, w_buf), lambda b, j: (b * per_seq + j, 0, 0))
    win_spec = pl.BlockSpec((1, B_WIDTH, tm), lambda b, j: (b, 0, jnp.maximum(j - first, 0)))
    half = lambda dt: jax.ShapeDtypeStruct((n, A_WIDTH), dt)
    win = jax.ShapeDtypeStruct((bn, B_WIDTH, keep), F32)
    return pl.pallas_call(
        functools.partial(_proj_prompt_kernel, q_scale=q_scale),
        grid=(bn, per_seq),
        in_specs=_proj_in_specs(tm, row, tab, fixed) + [
            new_spec, new_spec, new_spec, buf_spec, buf_spec,
            pl.BlockSpec(cnt_old.shape, fixed), pl.BlockSpec(cnt_new.shape, fixed)],
        out_specs=[pl.BlockSpec((tm, A_WIDTH), row)] * 5 + [win_spec, win_spec, new_spec],
        out_shape=[half(BF16)] * 5 + [win, win, jax.ShapeDtypeStruct((bd * t_new, B_WIDTH), BF16)],
        compiler_params=pltpu.CompilerParams(
            dimension_semantics=("arbitrary", "arbitrary"), vmem_limit_bytes=VMEM_LIMIT),
        name="proj_prompt",
    )(x, norm_g, w_bf, ln_g, ln_b, *tabs, *mix, qs, k_new, v_new, cache_kt, cache_vt, cnt_old, cnt_new)


def _rope_tables(pos):
    pos = np.asarray(pos, np.float64)
    inv = ROPE_THETA ** (-np.arange(0, ROT_DIM, 2, dtype=np.float64) / ROT_DIM)
    ang = pos[:, None] * inv[None, :]
    cos, sin = np.cos(ang), np.sin(ang)
    half = ROT_DIM // 2
    ones = np.ones((pos.shape[0], HEAD_DIM - ROT_DIM))
    zeros = np.zeros((pos.shape[0], HEAD_DIM - half))
    c = np.concatenate([cos, cos, ones], axis=1)
    sn = np.concatenate([-sin, zeros], axis=1)
    sp = np.concatenate([zeros[:, :half], sin, zeros[:, :HEAD_DIM - ROT_DIM]], axis=1)
    rep = lambda t: jnp.asarray(np.tile(t, (1, HEADS_PER_SLAB)).astype(np.float32))
    return rep(c), rep(sn), rep(sp)


ATT_TILE = WIN_MAX
UNIT_UNROLL = 16
WIDEN_ROWS = 512
REGROUP = 4


def _attn_kernel(qb_ref, kb_ref, vb_ref, o_ref, q4_ref, k4_ref, v4_ref, stage_ref, osc, lsc, bias_ref):
    seq = qb_ref.shape[1]
    quarter = seq // REGROUP

    def widen(i, carry):
        src = pl.ds(pl.multiple_of(i * WIDEN_ROWS, WIDEN_ROWS), WIDEN_ROWS)
        per_class = WIDEN_ROWS // REGROUP
        for n, (b_ref, w_ref) in enumerate(((qb_ref, q4_ref), (kb_ref, k4_ref), (vb_ref, v4_ref))):
            stage_ref[n] = b_ref[0, src, :].astype(F32)
            for e in range(REGROUP):
                dst = pl.ds(pl.multiple_of(e * quarter + i * per_class, per_class), per_class)
                w_ref[dst, :] = stage_ref[n, pl.ds(e, per_class, stride=REGROUP), :]
        return carry

    lax.fori_loop(0, seq // WIDEN_ROWS, widen, 0)
    head0 = lax.broadcasted_iota(jnp.int32, (BAND, LANES), 1) < HEAD_DIM
    ii = lax.broadcasted_iota(jnp.int32, (2 * BAND, 2 * BAND), 0) % BAND
    jj = lax.broadcasted_iota(jnp.int32, (2 * BAND, 2 * BAND), 1)
    in_window = (jj <= ii + BAND) & (jj >= ii)
    bias_ref[1] = jnp.where(in_window, 0.0, -jnp.inf)
    bias_ref[0] = jnp.where(in_window & (jj >= BAND), 0.0, -jnp.inf)
    ones = jnp.ones((2 * BAND, LANES), BF16)

    def unit(c, dil, cls, m0, t):
        mp = jnp.maximum(m0 - BAND, 0)

        def block(b_ref, w_ref, start):
            if dil == 1:
                return b_ref[0, pl.ds(pl.multiple_of(start, BAND), BAND), :]
            if dil == REGROUP:
                rows = pl.ds(pl.multiple_of(cls * quarter + start, BAND), BAND)
            else:
                sub = cls // REGROUP
                step = dil // REGROUP
                rows = pl.ds((cls - sub * REGROUP) * quarter + sub + step * start, BAND, stride=step)
            return w_ref[rows, :].astype(BF16)

        q = block(qb_ref, q4_ref, m0)
        kk = jnp.concatenate([block(kb_ref, k4_ref, mp), block(kb_ref, k4_ref, m0)], axis=0)
        vv = jnp.concatenate([block(vb_ref, v4_ref, mp), block(vb_ref, v4_ref, m0)], axis=0)
        vv1 = jnp.concatenate([vv, ones], axis=1)
        zero = jnp.zeros_like(q)
        qs = jnp.concatenate([jnp.where(head0, q, zero), jnp.where(head0, zero, q)], axis=0)
        s = lax.dot_general(qs, kk, (((1,), (1,)), ((), ())), preferred_element_type=F32)
        s = s + bias_ref[jnp.where(m0 > 0, 1, 0)]
        m = jnp.max(s, axis=1, keepdims=True)
        p = jnp.exp2(s - m)
        pv = jnp.dot(p.astype(BF16), vv1, preferred_element_type=F32)
        o = jnp.where(head0, pv[:BAND, :LANES], pv[BAND:, :LANES])
        l = jnp.where(head0, pv[:BAND, LANES:], pv[BAND:, LANES:])
        mm = jnp.where(head0, jnp.broadcast_to(m[:BAND], (BAND, LANES)),
                       jnp.broadcast_to(m[BAND:], (BAND, LANES)))
        row0 = cls + dil * m0 - t * ATT_TILE
        osc[c, pl.ds(row0, BAND, stride=dil), :] = o / l
        lsc[c, pl.ds(row0, BAND, stride=dil), :] = mm + jnp.log2(l)

    units = ATT_TILE // BAND

    def tile_body(t, carry):
        for c, (_, dil) in enumerate(CONFIGS):
            blocks = units // dil

            def body(u, carry2, c=c, dil=dil, blocks=blocks):
                cls = u // blocks
                nb = u - cls * blocks
                unit(c, dil, cls, t * (ATT_TILE // dil) + nb * BAND, t)
                return carry2

            lax.fori_loop(0, units, body, 0, unroll=UNIT_UNROLL)

        rows_per_step = 256
        for r in range(ATT_TILE // rows_per_step):
            sl = pl.ds(r * rows_per_step, rows_per_step)
            ls = [lsc[c, sl, :] for c in range(len(CONFIGS))]
            mx = jnp.maximum(jnp.maximum(ls[0], ls[1]), ls[2])
            ws = [jnp.exp2(l - mx) for l in ls]
            num = ws[0] * osc[0, sl, :] + ws[1] * osc[1, sl, :] + ws[2] * osc[2, sl, :]
            dst = pl.ds(pl.multiple_of(t * ATT_TILE + r * rows_per_step, rows_per_step), rows_per_step)
            o_ref[0, dst, :] = (num / (ws[0] + ws[1] + ws[2])).astype(o_ref.dtype)
        return carry

    lax.fori_loop(0, seq // ATT_TILE, tile_body, 0)


def _attn(q, k, v):
    bn, seq, _ = q.shape
    spec = pl.BlockSpec((1, seq, LANES), lambda b, s: (b, 0, s))
    return pl.pallas_call(
        _attn_kernel,
        grid=(bn, N_SLABS),
        in_specs=[spec, spec, spec],
        out_specs=spec,
        out_shape=jax.ShapeDtypeStruct((bn, seq, B_WIDTH), BF16),
        scratch_shapes=[pltpu.VMEM((seq, LANES), F32)] * 3 + [
            pltpu.VMEM((3, WIDEN_ROWS, LANES), F32),
            pltpu.VMEM((len(CONFIGS), ATT_TILE, LANES), F32),
            pltpu.VMEM((len(CONFIGS), ATT_TILE, LANES), F32),
            pltpu.VMEM((2, 2 * BAND, 2 * BAND), F32)],
        compiler_params=pltpu.CompilerParams(
            dimension_semantics=("arbitrary", "arbitrary"), vmem_limit_bytes=VMEM_LIMIT),
        name="attn",
    )(q, k, v)


NEW_PAD = 128


def _sample_counts(w_buf, t_new):
    tq = np.arange(t_new)[:, None]

    def count(idx, exists):
        dist = w_buf + tq - idx[None, :]
        cnt = np.zeros(dist.shape, np.float64)
        for win, dil in CONFIGS:
            cnt += ((dist >= 0) & (dist <= win) & (dist % dil == 0) & exists[None, :])
        with np.errstate(divide="ignore"):
            bias = np.log2(cnt).astype(np.float32)
        return np.tile(bias, (HEADS, 1))

    old = np.arange(w_buf)
    new = np.arange(w_buf, w_buf + NEW_PAD)
    return count(old, old >= 0), count(new, new < w_buf + t_new)


def _sample_attention(q_ref, kn_ref, vn_ref, kt_ref, vt_ref, cnt_old_ref, cnt_new_ref, o_ref):
    t_new = q_ref.shape[0]
    rows = HEADS * t_new
    qrep = jnp.concatenate([q_ref[...]] * HEADS, axis=0)
    rh = lax.broadcasted_iota(jnp.int32, (rows, B_WIDTH), 0) // t_new
    lh = lax.broadcasted_iota(jnp.int32, (rows, B_WIDTH), 1) // HEAD_DIM
    own = rh == lh
    qbd = jnp.where(own, qrep, 0.0).astype(BF16)
    pad = jnp.zeros((NEW_PAD - t_new, B_WIDTH), F32)
    kn = jnp.concatenate([kn_ref[...], pad], axis=0).astype(BF16)
    vn = jnp.concatenate([vn_ref[...], pad], axis=0).astype(BF16)
    s_old = jnp.dot(qbd, kt_ref[0].astype(BF16), preferred_element_type=F32)
    s_new = lax.dot_general(qbd, kn, (((1,), (1,)), ((), ())), preferred_element_type=F32)
    s_old = s_old + cnt_old_ref[...]
    s_new = s_new + cnt_new_ref[...]
    m = jnp.maximum(jnp.max(s_old, axis=1, keepdims=True), jnp.max(s_new, axis=1, keepdims=True))
    p_old = jnp.exp2(s_old - m)
    p_new = jnp.exp2(s_new - m)
    l = jnp.sum(p_old, axis=1, keepdims=True) + jnp.sum(p_new, axis=1, keepdims=True)
    o = lax.dot_general(p_old.astype(BF16), vt_ref[0].astype(BF16), (((1,), (1,)), ((), ())),
                        preferred_element_type=F32)
    o = (o + jnp.dot(p_new.astype(BF16), vn, preferred_element_type=F32)) / l
    o = jnp.where(own, o, 0.0)
    acc = o[0:t_new]
    for h in range(1, HEADS):
        acc = acc + o[h * t_new:(h + 1) * t_new]
    o_ref[...] = acc.astype(o_ref.dtype)


def _transpose_kernel(k_ref, v_ref, kt_ref, vt_ref):
    kt_ref[0] = k_ref[...].T
    vt_ref[0] = v_ref[...].T


def _transpose_blocks(k, v, grid, src_block, src_map, out_dims, dst_map):
    src = pl.BlockSpec(src_block, src_map)
    dst = pl.BlockSpec((1, src_block[1], src_block[0]), dst_map)
    shape = jax.ShapeDtypeStruct(out_dims, F32)
    return pl.pallas_call(
        _transpose_kernel, grid=grid, in_specs=[src, src], out_specs=[dst, dst],
        out_shape=[shape, shape],
        compiler_params=pltpu.CompilerParams(dimension_semantics=("arbitrary",) * len(grid)),
        name="transpose",
    )(k, v)


def _finish_kernel(x_ref, ha_ref, b_ref, sgb_ref, gnb_ref, wout_ref, fg_ref, y_ref):
    hb = _rms(b_ref[...].astype(F32) * sgb_ref[...].astype(F32), gnb_ref[...])
    hcat = jnp.concatenate([ha_ref[...], hb.astype(BF16)], axis=1)
    y = x_ref[...] + jnp.dot(hcat, wout_ref[...], preferred_element_type=F32)
    y_ref[...] = _rms(y, fg_ref[...])


def _finish(x, ha, b, sgb, gn_b, wout_bf, final_g, tm):
    n = x.shape[0]
    row = lambda i: (i, 0)
    fixed = lambda i: (0, 0)
    half = pl.BlockSpec((tm, A_WIDTH), row)
    return pl.pallas_call(
        _finish_kernel,
        grid=(n // tm,),
        in_specs=[
            pl.BlockSpec((tm, D_MODEL), row), half, half, half,
            pl.BlockSpec((1, B_WIDTH), fixed),
            pl.BlockSpec((D_MODEL, D_MODEL), fixed),
            pl.BlockSpec((1, D_MODEL), fixed),
        ],
        out_specs=pl.BlockSpec((tm, D_MODEL), row),
        out_shape=jax.ShapeDtypeStruct((n, D_MODEL), F32),
        compiler_params=pltpu.CompilerParams(
            dimension_semantics=("arbitrary",), vmem_limit_bytes=VMEM_LIMIT),
        name="finish",
    )(x, ha, b, sgb, gn_b, wout_bf, final_g)


def kernel(x_prompt, x_sample, cache_k_win, cache_v_win, norm_g, w_in, ln_v_g, ln_v_b,
           w_spatial, b_spatial, gn_a, gn_b, w_out, final_g):
    bp, seq, _ = x_prompt.shape
    bd, t_new, _ = x_sample.shape
    past_len = seq
    w_in_bf = w_in.astype(BF16)
    w_out_bf = w_out.astype(BF16)
    ng = norm_g.reshape(1, D_MODEL)
    lg, lb = ln_v_g.reshape(1, A_WIDTH), ln_v_b.reshape(1, A_WIDTH)
    ga, gb = gn_a.reshape(1, A_WIDTH), gn_b.reshape(1, B_WIDTH)
    fg = final_g.reshape(1, D_MODEL)
    tm_sample = 512
    tm_finish = 1024

    xs = x_sample.reshape(bd * t_new, D_MODEL)
    tabs_s = _rope_tables(past_len + np.tile(np.arange(t_new), bd))
    seqs_per_chunk = CHUNK // t_new
    eye = jnp.eye(seqs_per_chunk, dtype=w_spatial.dtype)
    wsp_s = jax.vmap(lambda w: jnp.kron(eye, w))(w_spatial[:, :t_new, :t_new])
    bsp_s = jnp.tile(b_spatial[:, :t_new], (1, seqs_per_chunk)).reshape(A_GROUPS, CHUNK, 1)
    q_scale = LOG2E * HEAD_DIM ** -0.5
    vas, has, sgbs, qs, ks, vs = _proj_sample(xs, ng, w_in_bf, lg, lb, tabs_s, (wsp_s, bsp_s, ga),
                                              tm_sample, q_scale)

    w_buf = cache_k_win.shape[1]
    cache_kt = cache_k_win.transpose(0, 2, 3, 1).reshape(bd, B_WIDTH, w_buf)
    cache_vt = cache_v_win.transpose(0, 2, 3, 1).reshape(bd, B_WIDTH, w_buf)
    xp = x_prompt.reshape(bp * seq, D_MODEL)
    tabs_p = _rope_tables(np.arange(seq))
    w_keep = min(WIN_MAX, seq)
    bsp = b_spatial.reshape(A_GROUPS, CHUNK, 1)
    ha, sgb, q, k, v, k_win_t, v_win_t, b_s = _proj_prompt(
        xp, bp, seq, ng, w_in_bf, lg, lb, tabs_p, (w_spatial, bsp, ga), q_scale,
        w_keep, qs, ks, vs, cache_kt, cache_vt, t_new)

    b_attn = _attn(q.reshape(bp, seq, B_WIDTH), k.reshape(bp, seq, B_WIDTH),
                   v.reshape(bp, seq, B_WIDTH))
    y_prompt = _finish(xp, ha, b_attn.reshape(bp * seq, B_WIDTH), sgb, gb, w_out_bf, fg,
                       tm_finish).reshape(bp, seq, D_MODEL)
    k_win = k_win_t.reshape(bp, HEADS, HEAD_DIM, w_keep).transpose(0, 3, 1, 2)
    v_win = v_win_t.reshape(bp, HEADS, HEAD_DIM, w_keep).transpose(0, 3, 1, 2)

    k_new_t, v_new_t = _transpose_blocks(
        ks.reshape(bd, t_new * B_WIDTH), vs.reshape(bd, t_new * B_WIDTH), (t_new,),
        (bd, B_WIDTH), lambda t: (0, t), (t_new, B_WIDTH, bd), lambda t: (t, 0, 0))
    k_new = k_new_t.reshape(t_new, HEADS, HEAD_DIM, bd).transpose(3, 0, 1, 2)
    v_new = v_new_t.reshape(t_new, HEADS, HEAD_DIM, bd).transpose(3, 0, 1, 2)
    y_sample = _finish(xs, has, b_s, sgbs, gb, w_out_bf, fg, tm_finish).reshape(bd, t_new, D_MODEL)

    return (y_prompt, y_sample, k_win, v_win, k_new, v_new, vas.reshape(bd, t_new, A_WIDTH))
```

```python
import functools

import numpy as np
import jax
import jax.numpy as jnp
from jax import lax
from jax.experimental import pallas as pl
from jax.experimental.pallas import tpu as pltpu

D_MODEL = 1024
A_WIDTH = 512
B_WIDTH = 512
A_GROUPS = 4
CHUNK = 128
HEADS = 8
HEAD_DIM = 64
ROT_DIM = 16
ROPE_THETA = 500000.0
CONFIGS = ((128, 1), (512, 4), (2048, 16))
BAND = 128
WIN_MAX = 2048
EPS = 1e-6
LOG2E = 1.4426950408889634
IN_COLS = 3 * A_WIDTH + 4 * B_WIDTH

LANES = 128
HEADS_PER_SLAB = LANES // HEAD_DIM
N_SLABS = B_WIDTH // LANES
VMEM_LIMIT = 56 * 1024 * 1024

F32 = jnp.float32
BF16 = jnp.bfloat16


def _rms(x, g):
    return x * lax.rsqrt(jnp.mean(x * x, axis=-1, keepdims=True) + EPS) * g


def _silu(x):
    return x / (1.0 + jnp.exp(-x))


def _project_tile(x_ref, ng_ref, w_ref, lng_ref, lnb_ref, c_ref, sn_ref, sp_ref, wsp_ref, bsp_ref,
                  gna_ref, va_ref, ha_ref, sgb_ref, q_ref, k_ref, v_ref, q_scale, emit_kv=None):
    h = _rms(x_ref[...], ng_ref[...]).astype(BF16)
    nc = x_ref.shape[0] // CHUNK

    def col(g):
        return jnp.dot(h, w_ref[:, g * A_WIDTH:(g + 1) * A_WIDTH], preferred_element_type=F32)

    def rope(z, scale):
        c, sn, sp = c_ref[...], sn_ref[...], sp_ref[...]
        outs = []
        for s in range(N_SLABS):
            zs = z[:, s * LANES:(s + 1) * LANES]
            up = pltpu.roll(zs, LANES - ROT_DIM // 2, 1)
            dn = pltpu.roll(zs, ROT_DIM // 2, 1)
            r = zs * c + up * sn + dn * sp
            outs.append(r * scale if scale != 1.0 else r)
        return jnp.concatenate(outs, axis=1)

    va = jax.nn.gelu(col(1))
    mu = jnp.mean(va, axis=-1, keepdims=True)
    vc = va - mu
    va = vc * lax.rsqrt(jnp.mean(vc * vc, axis=-1, keepdims=True) + EPS) * lng_ref[...] + lnb_ref[...]
    if va_ref is not None:
        va_ref[...] = va
    va = va.astype(BF16)
    u = jax.nn.gelu(col(0))
    ri = lax.broadcasted_iota(jnp.int32, (CHUNK, CHUNK), 0)
    ci = lax.broadcasted_iota(jnp.int32, (CHUNK, CHUNK), 1)
    a_cols = []
    for g in range(A_GROUPS):
        wm = jnp.where(ri >= ci, wsp_ref[g], 0.0).astype(BF16)
        lanes = slice(g * CHUNK, (g + 1) * CHUNK)
        rhs = jnp.concatenate([va[c * CHUNK:(c + 1) * CHUNK, lanes] for c in range(nc)], axis=1)
        s = jnp.dot(wm, rhs, preferred_element_type=F32) + bsp_ref[g]
        s = jnp.concatenate([s[:, c * CHUNK:(c + 1) * CHUNK] for c in range(nc)], axis=0)
        a_cols.append(u[:, lanes] * s)
    a = jnp.concatenate(a_cols, axis=1)
    ha_ref[...] = _rms(a * _silu(col(2)), gna_ref[...]).astype(ha_ref.dtype)
    q_ref[...] = rope(col(3), q_scale).astype(q_ref.dtype)
    k = rope(col(4), 1.0)
    v = col(5)
    k_ref[...] = k.astype(k_ref.dtype)
    v_ref[...] = v.astype(v_ref.dtype)
    if emit_kv is not None:
        emit_kv(k, v)
    sgb_ref[...] = _silu(col(6)).astype(sgb_ref.dtype)


def _proj_sample_kernel(x_ref, ng_ref, w_ref, lng_ref, lnb_ref, c_ref, sn_ref, sp_ref, wsp_ref, bsp_ref,
                        gna_ref, va_ref, ha_ref, sgb_ref, q_ref, k_ref, v_ref, *, q_scale):
    _project_tile(x_ref, ng_ref, w_ref, lng_ref, lnb_ref, c_ref, sn_ref, sp_ref, wsp_ref, bsp_ref,
                  gna_ref, va_ref, ha_ref, sgb_ref, q_ref, k_ref, v_ref, q_scale)


def _proj_prompt_kernel(x_ref, ng_ref, w_ref, lng_ref, lnb_ref, c_ref, sn_ref, sp_ref, wsp_ref, bsp_ref,
                        gna_ref, qs_ref, kn_ref, vn_ref, ktc_hbm, vtc_hbm, cnt_old_ref, cnt_new_ref,
                        ha_ref, sgb_ref, q_ref, k_ref, v_ref, kt_ref, vt_ref, bs_ref,
                        kbuf, vbuf, sem, *, q_scale):
    step = pl.program_id(0) * pl.num_programs(1) + pl.program_id(1)
    n_steps = pl.num_programs(0) * pl.num_programs(1)
    ahead = RING_SLOTS - 1

    def window_copies(seq_idx):
        slot = seq_idx % RING_SLOTS
        return (pltpu.make_async_copy(ktc_hbm.at[seq_idx], kbuf.at[slot], sem.at[0, slot]),
                pltpu.make_async_copy(vtc_hbm.at[seq_idx], vbuf.at[slot], sem.at[1, slot]))

    @pl.when(step == 0)
    def _():
        for first in range(ahead):
            for copy in window_copies(first):
                copy.start()

    @pl.when(step + ahead < n_steps)
    def _():
        for copy in window_copies(step + ahead):
            copy.start()

    for copy in window_copies(step):
        copy.wait()
    slot = step % RING_SLOTS
    _sample_attention(qs_ref, kn_ref, vn_ref, kbuf.at[slot], vbuf.at[slot], cnt_old_ref, cnt_new_ref,
                      bs_ref)

    def emit_kv(k, v):
        kt_ref[0] = k.T
        vt_ref[0] = v.T

    _project_tile(x_ref, ng_ref, w_ref, lng_ref, lnb_ref, c_ref, sn_ref, sp_ref, wsp_ref, bsp_ref,
                  gna_ref, None, ha_ref, sgb_ref, q_ref, k_ref, v_ref, q_scale, emit_kv)


def _proj_in_specs(tm, row, tab, fixed):
    return [
        pl.BlockSpec((tm, D_MODEL), row),
        pl.BlockSpec((1, D_MODEL), fixed),
        pl.BlockSpec((D_MODEL, IN_COLS), fixed),
        pl.BlockSpec((1, A_WIDTH), fixed),
        pl.BlockSpec((1, A_WIDTH), fixed),
        pl.BlockSpec((tm, LANES), tab),
        pl.BlockSpec((tm, LANES), tab),
        pl.BlockSpec((tm, LANES), tab),
        pl.BlockSpec((A_GROUPS, CHUNK, CHUNK), lambda *_: (0, 0, 0)),
        pl.BlockSpec((A_GROUPS, CHUNK, 1), lambda *_: (0, 0, 0)),
        pl.BlockSpec((1, A_WIDTH), fixed),
    ]


def _proj_sample(x, norm_g, w_bf, ln_g, ln_b, tabs, mix, tm, q_scale):
    n = x.shape[0]
    row = lambda i: (i, 0)
    fixed = lambda i: (0, 0)
    half = lambda dt: jax.ShapeDtypeStruct((n, A_WIDTH), dt)
    return pl.pallas_call(
        functools.partial(_proj_sample_kernel, q_scale=q_scale),
        grid=(n // tm,),
        in_specs=_proj_in_specs(tm, row, row, fixed),
        out_specs=[pl.BlockSpec((tm, A_WIDTH), row)] * 6,
        out_shape=[half(F32), half(BF16), half(BF16), half(F32), half(F32), half(F32)],
        compiler_params=pltpu.CompilerParams(
            dimension_semantics=("arbitrary",), vmem_limit_bytes=VMEM_LIMIT),
        name="proj_sample",
    )(x, norm_g, w_bf, ln_g, ln_b, *tabs, *mix)


def _proj_prompt(x, bn, seq, norm_g, w_bf, ln_g, ln_b, tabs, mix, q_scale, keep,
                 qs, k_new, v_new, cache_kt, cache_vt, t_new):
    n = bn * seq
    bd, _, w_buf = cache_kt.shape
    assert n % bd == 0, "one sample sequence per prompt tile"
    tm = n // bd
    assert tm % CHUNK == 0 and seq % tm == 0 and keep % tm == 0 and bd >= RING_SLOTS
    per_seq = seq // tm
    first = (seq - keep) // tm
    cnt_old, cnt_new = (jnp.asarray(c) for c in _sample_counts(w_buf, t_new))
    row = lambda b, j: (b * per_seq + j, 0)
    tab = lambda b, j: (j, 0)
    fixed = lambda b, j: (0, 0)
    new_spec = pl.BlockSpec((t_new, B_WIDTH), row)
    ring = pltpu.VMEM((RING_SLOTS, B_WIDTH, w_buf), F32)
    win_spec = pl.BlockSpec((1, B_WIDTH, tm), lambda b, j: (b, 0, jnp.maximum(j - first, 0)))
    half = lambda dt: jax.ShapeDtypeStruct((n, A_WIDTH), dt)
    win = jax.ShapeDtypeStruct((bn, B_WIDTH, keep), F32)
    return pl.pallas_call(
        functools.partial(_proj_prompt_kernel, q_scale=q_scale),
        grid=(bn, per_seq),
        in_specs=_proj_in_specs(tm, row, tab, fixed) + [
            new_spec, new_spec, new_spec,
            pl.BlockSpec(memory_space=pl.ANY), pl.BlockSpec(memory_space=pl.ANY),
            pl.BlockSpec(cnt_old.shape, fixed), pl.BlockSpec(cnt_new.shape, fixed)],
        out_specs=[pl.BlockSpec((tm, A_WIDTH), row)] * 5 + [win_spec, win_spec, new_spec],
        out_shape=[half(BF16)] * 5 + [win, win, jax.ShapeDtypeStruct((bd * t_new, B_WIDTH), BF16)],
        scratch_shapes=[ring, ring, pltpu.SemaphoreType.DMA((2, RING_SLOTS))],
        compiler_params=pltpu.CompilerParams(
            dimension_semantics=("arbitrary", "arbitrary"), vmem_limit_bytes=VMEM_LIMIT),
        name="proj_prompt",
    )(x, norm_g, w_bf, ln_g, ln_b, *tabs, *mix, qs, k_new, v_new, cache_kt, cache_vt, cnt_old, cnt_new)


def _rope_tables(pos):
    pos = np.asarray(pos, np.float64)
    inv = ROPE_THETA ** (-np.arange(0, ROT_DIM, 2, dtype=np.float64) / ROT_DIM)
    ang = pos[:, None] * inv[None, :]
    cos, sin = np.cos(ang), np.sin(ang)
    half = ROT_DIM // 2
    ones = np.ones((pos.shape[0], HEAD_DIM - ROT_DIM))
    zeros = np.zeros((pos.shape[0], HEAD_DIM - half))
    c = np.concatenate([cos, cos, ones], axis=1)
    sn = np.concatenate([-sin, zeros], axis=1)
    sp = np.concatenate([zeros[:, :half], sin, zeros[:, :HEAD_DIM - ROT_DIM]], axis=1)
    rep = lambda t: jnp.asarray(np.tile(t, (1, HEADS_PER_SLAB)).astype(np.float32))
    return rep(c), rep(sn), rep(sp)


ATT_TILE = WIN_MAX
UNIT_UNROLL = 16
WIDEN_ROWS = 512
REGROUP = 4


def _attn_kernel(qb_ref, kb_ref, vb_ref, o_ref, q4_ref, k4_ref, v4_ref, stage_ref, osc, lsc, bias_ref):
    seq = qb_ref.shape[1]
    quarter = seq // REGROUP

    def widen(i, carry):
        src = pl.ds(pl.multiple_of(i * WIDEN_ROWS, WIDEN_ROWS), WIDEN_ROWS)
        per_class = WIDEN_ROWS // REGROUP
        for n, (b_ref, w_ref) in enumerate(((qb_ref, q4_ref), (kb_ref, k4_ref), (vb_ref, v4_ref))):
            stage_ref[n] = b_ref[0, src, :].astype(F32)
            for e in range(REGROUP):
                dst = pl.ds(pl.multiple_of(e * quarter + i * per_class, per_class), per_class)
                w_ref[dst, :] = stage_ref[n, pl.ds(e, per_class, stride=REGROUP), :]
        return carry

    lax.fori_loop(0, seq // WIDEN_ROWS, widen, 0)
    head0 = lax.broadcasted_iota(jnp.int32, (BAND, LANES), 1) < HEAD_DIM
    ii = lax.broadcasted_iota(jnp.int32, (2 * BAND, 2 * BAND), 0) % BAND
    jj = lax.broadcasted_iota(jnp.int32, (2 * BAND, 2 * BAND), 1)
    in_window = (jj <= ii + BAND) & (jj >= ii)
    bias_ref[1] = jnp.where(in_window, 0.0, -jnp.inf)
    bias_ref[0] = jnp.where(in_window & (jj >= BAND), 0.0, -jnp.inf)
    ones = jnp.ones((2 * BAND, LANES), BF16)

    def unit(c, dil, cls, m0, t):
        mp = jnp.maximum(m0 - BAND, 0)

        def block(b_ref, w_ref, start):
            if dil == 1:
                return b_ref[0, pl.ds(pl.multiple_of(start, BAND), BAND), :]
            if dil == REGROUP:
                rows = pl.ds(pl.multiple_of(cls * quarter + start, BAND), BAND)
            else:
                sub = cls // REGROUP
                step = dil // REGROUP
                rows = pl.ds((cls - sub * REGROUP) * quarter + sub + step * start, BAND, stride=step)
            return w_ref[rows, :].astype(BF16)

        q = block(qb_ref, q4_ref, m0)
        kk = jnp.concatenate([block(kb_ref, k4_ref, mp), block(kb_ref, k4_ref, m0)], axis=0)
        vv = jnp.concatenate([block(vb_ref, v4_ref, mp), block(vb_ref, v4_ref, m0)], axis=0)
        vv1 = jnp.concatenate([vv, ones], axis=1)
        zero = jnp.zeros_like(q)
        qs = jnp.concatenate([jnp.where(head0, q, zero), jnp.where(head0, zero, q)], axis=0)
        s = lax.dot_general(qs, kk, (((1,), (1,)), ((), ())), preferred_element_type=F32)
        s = s + bias_ref[jnp.where(m0 > 0, 1, 0)]
        m = jnp.max(s, axis=1, keepdims=True)
        p = jnp.exp2(s - m)
        pv = jnp.dot(p.astype(BF16), vv1, preferred_element_type=F32)
        o = jnp.where(head0, pv[:BAND, :LANES], pv[BAND:, :LANES])
        l = jnp.where(head0, pv[:BAND, LANES:], pv[BAND:, LANES:])
        mm = jnp.where(head0, jnp.broadcast_to(m[:BAND], (BAND, LANES)),
                       jnp.broadcast_to(m[BAND:], (BAND, LANES)))
        row0 = cls + dil * m0 - t * ATT_TILE
        osc[c, pl.ds(row0, BAND, stride=dil), :] = o / l
        lsc[c, pl.ds(row0, BAND, stride=dil), :] = mm + jnp.log2(l)

    units = ATT_TILE // BAND

    def tile_body(t, carry):
        for c, (_, dil) in enumerate(CONFIGS):
            blocks = units // dil

            def body(u, carry2, c=c, dil=dil, blocks=blocks):
                cls = u // blocks
                nb = u - cls * blocks
                unit(c, dil, cls, t * (ATT_TILE // dil) + nb * BAND, t)
                return carry2

            lax.fori_loop(0, units, body, 0, unroll=UNIT_UNROLL)

        rows_per_step = 256
        for r in range(ATT_TILE // rows_per_step):
            sl = pl.ds(r * rows_per_step, rows_per_step)
            ls = [lsc[c, sl, :] for c in range(len(CONFIGS))]
            mx = jnp.maximum(jnp.maximum(ls[0], ls[1]), ls[2])
            ws = [jnp.exp2(l - mx) for l in ls]
            num = ws[0] * osc[0, sl, :] + ws[1] * osc[1, sl, :] + ws[2] * osc[2, sl, :]
            dst = pl.ds(pl.multiple_of(t * ATT_TILE + r * rows_per_step, rows_per_step), rows_per_step)
            o_ref[0, dst, :] = (num / (ws[0] + ws[1] + ws[2])).astype(o_ref.dtype)
        return carry

    lax.fori_loop(0, seq // ATT_TILE, tile_body, 0)


def _attn(q, k, v):
    bn, seq, _ = q.shape
    spec = pl.BlockSpec((1, seq, LANES), lambda b, s: (b, 0, s))
    return pl.pallas_call(
        _attn_kernel,
        grid=(bn, N_SLABS),
        in_specs=[spec, spec, spec],
        out_specs=spec,
        out_shape=jax.ShapeDtypeStruct((bn, seq, B_WIDTH), BF16),
        scratch_shapes=[pltpu.VMEM((seq, LANES), F32)] * 3 + [
            pltpu.VMEM((3, WIDEN_ROWS, LANES), F32),
            pltpu.VMEM((len(CONFIGS), ATT_TILE, LANES), F32),
            pltpu.VMEM((len(CONFIGS), ATT_TILE, LANES), F32),
            pltpu.VMEM((2, 2 * BAND, 2 * BAND), F32)],
        compiler_params=pltpu.CompilerParams(
            dimension_semantics=("arbitrary", "arbitrary"), vmem_limit_bytes=VMEM_LIMIT),
        name="attn",
    )(q, k, v)


NEW_PAD = 128
RING_SLOTS = 3


def _sample_counts(w_buf, t_new):
    tq = np.arange(t_new)[:, None]

    def count(idx, exists):
        dist = w_buf + tq - idx[None, :]
        cnt = np.zeros(dist.shape, np.float64)
        for win, dil in CONFIGS:
            cnt += ((dist >= 0) & (dist <= win) & (dist % dil == 0) & exists[None, :])
        with np.errstate(divide="ignore"):
            bias = np.log2(cnt).astype(np.float32)
        return np.tile(bias, (HEADS, 1))

    old = np.arange(w_buf)
    new = np.arange(w_buf, w_buf + NEW_PAD)
    return count(old, old >= 0), count(new, new < w_buf + t_new)


def _sample_attention(q_ref, kn_ref, vn_ref, kt_ref, vt_ref, cnt_old_ref, cnt_new_ref, o_ref):
    t_new = q_ref.shape[0]
    rows = HEADS * t_new
    qrep = jnp.concatenate([q_ref[...]] * HEADS, axis=0)
    rh = lax.broadcasted_iota(jnp.int32, (rows, B_WIDTH), 0) // t_new
    lh = lax.broadcasted_iota(jnp.int32, (rows, B_WIDTH), 1) // HEAD_DIM
    own = rh == lh
    qbd = jnp.where(own, qrep, 0.0).astype(BF16)
    pad = jnp.zeros((NEW_PAD - t_new, B_WIDTH), F32)
    kn = jnp.concatenate([kn_ref[...], pad], axis=0).astype(BF16)
    vn = jnp.concatenate([vn_ref[...], pad], axis=0).astype(BF16)
    s_old = jnp.dot(qbd, kt_ref[...].astype(BF16), preferred_element_type=F32)
    s_new = lax.dot_general(qbd, kn, (((1,), (1,)), ((), ())), preferred_element_type=F32)
    s_old = s_old + cnt_old_ref[...]
    s_new = s_new + cnt_new_ref[...]
    m = jnp.maximum(jnp.max(s_old, axis=1, keepdims=True), jnp.max(s_new, axis=1, keepdims=True))
    p_old = jnp.exp2(s_old - m)
    p_new = jnp.exp2(s_new - m)
    l = jnp.sum(p_old, axis=1, keepdims=True) + jnp.sum(p_new, axis=1, keepdims=True)
    o = lax.dot_general(p_old.astype(BF16), vt_ref[...].astype(BF16), (((1,), (1,)), ((), ())),
                        preferred_element_type=F32)
    o = (o + jnp.dot(p_new.astype(BF16), vn, preferred_element_type=F32)) / l
    o = jnp.where(own, o, 0.0)
    acc = o[0:t_new]
    for h in range(1, HEADS):
        acc = acc + o[h * t_new:(h + 1) * t_new]
    o_ref[...] = acc.astype(o_ref.dtype)


def _transpose_kernel(k_ref, v_ref, kt_ref, vt_ref):
    kt_ref[0] = k_ref[...].T
    vt_ref[0] = v_ref[...].T


def _transpose_blocks(k, v, grid, src_block, src_map, out_dims, dst_map):
    src = pl.BlockSpec(src_block, src_map)
    dst = pl.BlockSpec((1, src_block[1], src_block[0]), dst_map)
    shape = jax.ShapeDtypeStruct(out_dims, F32)
    return pl.pallas_call(
        _transpose_kernel, grid=grid, in_specs=[src, src], out_specs=[dst, dst],
        out_shape=[shape, shape],
        compiler_params=pltpu.CompilerParams(dimension_semantics=("arbitrary",) * len(grid)),
        name="transpose",
    )(k, v)


def _finish_kernel(x_ref, ha_ref, b_ref, sgb_ref, gnb_ref, wout_ref, fg_ref, y_ref):
    hb = _rms(b_ref[...].astype(F32) * sgb_ref[...].astype(F32), gnb_ref[...])
    hcat = jnp.concatenate([ha_ref[...], hb.astype(BF16)], axis=1)
    y = x_ref[...] + jnp.dot(hcat, wout_ref[...], preferred_element_type=F32)
    y_ref[...] = _rms(y, fg_ref[...])


def _finish(x, ha, b, sgb, gn_b, wout_bf, final_g, tm):
    n = x.shape[0]
    row = lambda i: (i, 0)
    fixed = lambda i: (0, 0)
    half = pl.BlockSpec((tm, A_WIDTH), row)
    return pl.pallas_call(
        _finish_kernel,
        grid=(n // tm,),
        in_specs=[
            pl.BlockSpec((tm, D_MODEL), row), half, half, half,
            pl.BlockSpec((1, B_WIDTH), fixed),
            pl.BlockSpec((D_MODEL, D_MODEL), fixed),
            pl.BlockSpec((1, D_MODEL), fixed),
        ],
        out_specs=pl.BlockSpec((tm, D_MODEL), row),
        out_shape=jax.ShapeDtypeStruct((n, D_MODEL), F32),
        compiler_params=pltpu.CompilerParams(
            dimension_semantics=("arbitrary",), vmem_limit_bytes=VMEM_LIMIT),
        name="finish",
    )(x, ha, b, sgb, gn_b, wout_bf, final_g)


def kernel(x_prompt, x_sample, cache_k_win, cache_v_win, norm_g, w_in, ln_v_g, ln_v_b,
           w_spatial, b_spatial, gn_a, gn_b, w_out, final_g):
    bp, seq, _ = x_prompt.shape
    bd, t_new, _ = x_sample.shape
    past_len = seq
    w_in_bf = w_in.astype(BF16)
    w_out_bf = w_out.astype(BF16)
    ng = norm_g.reshape(1, D_MODEL)
    lg, lb = ln_v_g.reshape(1, A_WIDTH), ln_v_b.reshape(1, A_WIDTH)
    ga, gb = gn_a.reshape(1, A_WIDTH), gn_b.reshape(1, B_WIDTH)
    fg = final_g.reshape(1, D_MODEL)
    tm_sample = 512
    tm_finish = 1024

    xs = x_sample.reshape(bd * t_new, D_MODEL)
    tabs_s = _rope_tables(past_len + np.tile(np.arange(t_new), bd))
    seqs_per_chunk = CHUNK // t_new
    eye = jnp.eye(seqs_per_chunk, dtype=w_spatial.dtype)
    wsp_s = jax.vmap(lambda w: jnp.kron(eye, w))(w_spatial[:, :t_new, :t_new])
    bsp_s = jnp.tile(b_spatial[:, :t_new], (1, seqs_per_chunk)).reshape(A_GROUPS, CHUNK, 1)
    q_scale = LOG2E * HEAD_DIM ** -0.5
    vas, has, sgbs, qs, ks, vs = _proj_sample(xs, ng, w_in_bf, lg, lb, tabs_s, (wsp_s, bsp_s, ga),
                                              tm_sample, q_scale)

    w_buf = cache_k_win.shape[1]
    cache_kt = cache_k_win.transpose(0, 2, 3, 1).reshape(bd, B_WIDTH, w_buf)
    cache_vt = cache_v_win.transpose(0, 2, 3, 1).reshape(bd, B_WIDTH, w_buf)
    xp = x_prompt.reshape(bp * seq, D_MODEL)
    tabs_p = _rope_tables(np.arange(seq))
    w_keep = min(WIN_MAX, seq)
    bsp = b_spatial.reshape(A_GROUPS, CHUNK, 1)
    ha, sgb, q, k, v, k_win_t, v_win_t, b_s = _proj_prompt(
        xp, bp, seq, ng, w_in_bf, lg, lb, tabs_p, (w_spatial, bsp, ga), q_scale,
        w_keep, qs, ks, vs, cache_kt, cache_vt, t_new)

    b_attn = _attn(q.reshape(bp, seq, B_WIDTH), k.reshape(bp, seq, B_WIDTH),
                   v.reshape(bp, seq, B_WIDTH))
    y_prompt = _finish(xp, ha, b_attn.reshape(bp * seq, B_WIDTH), sgb, gb, w_out_bf, fg,
                       tm_finish).reshape(bp, seq, D_MODEL)
    k_win = k_win_t.reshape(bp, HEADS, HEAD_DIM, w_keep).transpose(0, 3, 1, 2)
    v_win = v_win_t.reshape(bp, HEADS, HEAD_DIM, w_keep).transpose(0, 3, 1, 2)

    k_new_t, v_new_t = _transpose_blocks(
        ks.reshape(bd, t_new * B_WIDTH), vs.reshape(bd, t_new * B_WIDTH), (t_new,),
        (bd, B_WIDTH), lambda t: (0, t), (t_new, B_WIDTH, bd), lambda t: (t, 0, 0))
    k_new = k_new_t.reshape(t_new, HEADS, HEAD_DIM, bd).transpose(3, 0, 1, 2)
    v_new = v_new_t.reshape(t_new, HEADS, HEAD_DIM, bd).transpose(3, 0, 1, 2)
    y_sample = _finish(xs, has, b_s, sgbs, gb, w_out_bf, fg, tm_finish).reshape(bd, t_new, D_MODEL)

    return (y_prompt, y_sample, k_win, v_win, k_new, v_new, vas.reshape(bd, t_new, A_WIDTH))
```

```python
import functools

import numpy as np
import jax
import jax.numpy as jnp
from jax import lax
from jax.experimental import pallas as pl
from jax.experimental.pallas import tpu as pltpu

D_MODEL = 1024
A_WIDTH = 512
B_WIDTH = 512
A_GROUPS = 4
CHUNK = 128
HEADS = 8
HEAD_DIM = 64
ROT_DIM = 16
ROPE_THETA = 500000.0
CONFIGS = ((128, 1), (512, 4), (2048, 16))
BAND = 128
WIN_MAX = 2048
EPS = 1e-6
LOG2E = 1.4426950408889634
IN_COLS = 3 * A_WIDTH + 4 * B_WIDTH

LANES = 128
HEADS_PER_SLAB = LANES // HEAD_DIM
N_SLABS = B_WIDTH // LANES
VMEM_LIMIT = 56 * 1024 * 1024

F32 = jnp.float32
BF16 = jnp.bfloat16


def _rms(x, g):
    return x * lax.rsqrt(jnp.mean(x * x, axis=-1, keepdims=True) + EPS) * g


def _silu(x):
    return x / (1.0 + jnp.exp(-x))


RING_SLOTS = 3


def _ring_advance(copies, step, n_steps):
    ahead = RING_SLOTS - 1

    @pl.when(step == 0)
    def _():
        for first in range(min(ahead, n_steps)):
            for copy in copies(first):
                copy.start()

    @pl.when(step + ahead < n_steps)
    def _():
        for copy in copies(step + ahead):
            copy.start()

    for copy in copies(step):
        copy.wait()


def _project_tile(x_ref, ng_ref, w_ref, lng_ref, lnb_ref, c_ref, sn_ref, sp_ref, wsp_ref, bsp_ref,
                  gna_ref, va_ref, ha_ref, sgb_ref, q_ref, k_ref, v_ref, q_scale, emit_kv=None):
    h = _rms(x_ref[...], ng_ref[...]).astype(BF16)
    nc = x_ref.shape[0] // CHUNK

    def col(g):
        return jnp.dot(h, w_ref[:, g * A_WIDTH:(g + 1) * A_WIDTH], preferred_element_type=F32)

    def rope(z, scale):
        c, sn, sp = c_ref[...], sn_ref[...], sp_ref[...]
        outs = []
        for s in range(N_SLABS):
            zs = z[:, s * LANES:(s + 1) * LANES]
            up = pltpu.roll(zs, LANES - ROT_DIM // 2, 1)
            dn = pltpu.roll(zs, ROT_DIM // 2, 1)
            r = zs * c + up * sn + dn * sp
            outs.append(r * scale if scale != 1.0 else r)
        return jnp.concatenate(outs, axis=1)

    va = jax.nn.gelu(col(1))
    mu = jnp.mean(va, axis=-1, keepdims=True)
    vc = va - mu
    va = vc * lax.rsqrt(jnp.mean(vc * vc, axis=-1, keepdims=True) + EPS) * lng_ref[...] + lnb_ref[...]
    if va_ref is not None:
        va_ref[...] = va
    va = va.astype(BF16)
    u = jax.nn.gelu(col(0))
    ri = lax.broadcasted_iota(jnp.int32, (CHUNK, CHUNK), 0)
    ci = lax.broadcasted_iota(jnp.int32, (CHUNK, CHUNK), 1)
    a_cols = []
    for g in range(A_GROUPS):
        wm = jnp.where(ri >= ci, wsp_ref[g], 0.0).astype(BF16)
        lanes = slice(g * CHUNK, (g + 1) * CHUNK)
        rhs = jnp.concatenate([va[c * CHUNK:(c + 1) * CHUNK, lanes] for c in range(nc)], axis=1)
        s = jnp.dot(wm, rhs, preferred_element_type=F32) + bsp_ref[g]
        s = jnp.concatenate([s[:, c * CHUNK:(c + 1) * CHUNK] for c in range(nc)], axis=0)
        a_cols.append(u[:, lanes] * s)
    a = jnp.concatenate(a_cols, axis=1)
    ha_ref[...] = _rms(a * _silu(col(2)), gna_ref[...]).astype(ha_ref.dtype)
    q_ref[...] = rope(col(3), q_scale).astype(q_ref.dtype)
    k = rope(col(4), 1.0)
    v = col(5)
    k_ref[...] = k.astype(k_ref.dtype)
    v_ref[...] = v.astype(v_ref.dtype)
    if emit_kv is not None:
        emit_kv(k, v)
    sgb_ref[...] = _silu(col(6)).astype(sgb_ref.dtype)


def _proj_sample_kernel(x_ref, ng_ref, w_ref, lng_ref, lnb_ref, c_ref, sn_ref, sp_ref, wsp_ref, bsp_ref,
                        gna_ref, va_ref, ha_ref, sgb_ref, q_ref, k_ref, v_ref, *, q_scale):
    _project_tile(x_ref, ng_ref, w_ref, lng_ref, lnb_ref, c_ref, sn_ref, sp_ref, wsp_ref, bsp_ref,
                  gna_ref, va_ref, ha_ref, sgb_ref, q_ref, k_ref, v_ref, q_scale)


def _proj_prompt_kernel(x_ref, ng_ref, w_ref, lng_ref, lnb_ref, c_ref, sn_ref, sp_ref, wsp_ref, bsp_ref,
                        gna_ref, qs_ref, kn_ref, vn_ref, ktc_hbm, vtc_hbm, cnt_old_ref, cnt_new_ref,
                        ha_ref, sgb_ref, q_ref, k_ref, v_ref, kt_ref, vt_ref, bs_ref,
                        kbuf, vbuf, sem, *, q_scale, n_steps):
    step = pl.program_id(0) * pl.num_programs(1) + pl.program_id(1)

    def window_copies(seq_idx):
        slot = seq_idx % RING_SLOTS
        return (pltpu.make_async_copy(ktc_hbm.at[seq_idx], kbuf.at[slot], sem.at[0, slot]),
                pltpu.make_async_copy(vtc_hbm.at[seq_idx], vbuf.at[slot], sem.at[1, slot]))

    _ring_advance(window_copies, step, n_steps)
    slot = step % RING_SLOTS
    _sample_attention(qs_ref, kn_ref, vn_ref, kbuf.at[slot], vbuf.at[slot], cnt_old_ref, cnt_new_ref,
                      bs_ref)

    def emit_kv(k, v):
        kt_ref[0] = k.T
        vt_ref[0] = v.T

    _project_tile(x_ref, ng_ref, w_ref, lng_ref, lnb_ref, c_ref, sn_ref, sp_ref, wsp_ref, bsp_ref,
                  gna_ref, None, ha_ref, sgb_ref, q_ref, k_ref, v_ref, q_scale, emit_kv)


def _proj_in_specs(tm, row, tab, fixed):
    return [
        pl.BlockSpec((tm, D_MODEL), row),
        pl.BlockSpec((1, D_MODEL), fixed),
        pl.BlockSpec((D_MODEL, IN_COLS), fixed),
        pl.BlockSpec((1, A_WIDTH), fixed),
        pl.BlockSpec((1, A_WIDTH), fixed),
        pl.BlockSpec((tm, LANES), tab),
        pl.BlockSpec((tm, LANES), tab),
        pl.BlockSpec((tm, LANES), tab),
        pl.BlockSpec((A_GROUPS, CHUNK, CHUNK), lambda *_: (0, 0, 0)),
        pl.BlockSpec((A_GROUPS, CHUNK, 1), lambda *_: (0, 0, 0)),
        pl.BlockSpec((1, A_WIDTH), fixed),
    ]


def _proj_sample(x, norm_g, w_bf, ln_g, ln_b, tabs, mix, tm, q_scale):
    n = x.shape[0]
    row = lambda i: (i, 0)
    fixed = lambda i: (0, 0)
    half = lambda dt: jax.ShapeDtypeStruct((n, A_WIDTH), dt)
    return pl.pallas_call(
        functools.partial(_proj_sample_kernel, q_scale=q_scale),
        grid=(n // tm,),
        in_specs=_proj_in_specs(tm, row, row, fixed),
        out_specs=[pl.BlockSpec((tm, A_WIDTH), row)] * 6,
        out_shape=[half(F32), half(BF16), half(BF16), half(F32), half(F32), half(F32)],
        compiler_params=pltpu.CompilerParams(
            dimension_semantics=("arbitrary",), vmem_limit_bytes=VMEM_LIMIT),
        name="proj_sample",
    )(x, norm_g, w_bf, ln_g, ln_b, *tabs, *mix)


def _proj_prompt(x, bn, seq, norm_g, w_bf, ln_g, ln_b, tabs, mix, q_scale, keep,
                 qs, k_new, v_new, cache_kt, cache_vt, t_new):
    n = bn * seq
    bd, _, w_buf = cache_kt.shape
    assert n % bd == 0, "one sample sequence per prompt tile"
    tm = n // bd
    assert tm % CHUNK == 0 and seq % tm == 0 and keep % tm == 0
    per_seq = seq // tm
    first = (seq - keep) // tm
    cnt_old, cnt_new = (jnp.asarray(c) for c in _sample_counts(w_buf, t_new))
    row = lambda b, j: (b * per_seq + j, 0)
    tab = lambda b, j: (j, 0)
    fixed = lambda b, j: (0, 0)
    new_spec = pl.BlockSpec((t_new, B_WIDTH), row)
    ring = pltpu.VMEM((RING_SLOTS, B_WIDTH, w_buf), F32)
    win_spec = pl.BlockSpec((1, B_WIDTH, tm), lambda b, j: (b, 0, jnp.maximum(j - first, 0)))
    half = lambda dt: jax.ShapeDtypeStruct((n, A_WIDTH), dt)
    win = jax.ShapeDtypeStruct((bn, B_WIDTH, keep), F32)
    return pl.pallas_call(
        functools.partial(_proj_prompt_kernel, q_scale=q_scale, n_steps=bd),
        grid=(bn, per_seq),
        in_specs=_proj_in_specs(tm, row, tab, fixed) + [
            new_spec, new_spec, new_spec,
            pl.BlockSpec(memory_space=pl.ANY), pl.BlockSpec(memory_space=pl.ANY),
            pl.BlockSpec(cnt_old.shape, fixed), pl.BlockSpec(cnt_new.shape, fixed)],
        out_specs=[pl.BlockSpec((tm, A_WIDTH), row)] * 5 + [win_spec, win_spec, new_spec],
        out_shape=[half(BF16)] * 5 + [win, win, jax.ShapeDtypeStruct((bd * t_new, B_WIDTH), BF16)],
        scratch_shapes=[ring, ring, pltpu.SemaphoreType.DMA((2, RING_SLOTS))],
        compiler_params=pltpu.CompilerParams(
            dimension_semantics=("arbitrary", "arbitrary"), vmem_limit_bytes=VMEM_LIMIT),
        name="proj_prompt",
    )(x, norm_g, w_bf, ln_g, ln_b, *tabs, *mix, qs, k_new, v_new, cache_kt, cache_vt, cnt_old, cnt_new)


def _rope_tables(pos):
    pos = np.asarray(pos, np.float64)
    inv = ROPE_THETA ** (-np.arange(0, ROT_DIM, 2, dtype=np.float64) / ROT_DIM)
    ang = pos[:, None] * inv[None, :]
    cos, sin = np.cos(ang), np.sin(ang)
    half = ROT_DIM // 2
    ones = np.ones((pos.shape[0], HEAD_DIM - ROT_DIM))
    zeros = np.zeros((pos.shape[0], HEAD_DIM - half))
    c = np.concatenate([cos, cos, ones], axis=1)
    sn = np.concatenate([-sin, zeros], axis=1)
    sp = np.concatenate([zeros[:, :half], sin, zeros[:, :HEAD_DIM - ROT_DIM]], axis=1)
    rep = lambda t: jnp.asarray(np.tile(t, (1, HEADS_PER_SLAB)).astype(np.float32))
    return rep(c), rep(sn), rep(sp)


ATT_TILE = WIN_MAX
UNIT_UNROLL = 16
WIDEN_ROWS = 512
REGROUP = 4


def _attn_kernel(qb_ref, kb_ref, vb_ref, o_ref, q4_ref, k4_ref, v4_ref, stage_ref, osc, lsc, bias_ref):
    seq = qb_ref.shape[1]
    quarter = seq // REGROUP

    def widen(i, carry):
        src = pl.ds(pl.multiple_of(i * WIDEN_ROWS, WIDEN_ROWS), WIDEN_ROWS)
        per_class = WIDEN_ROWS // REGROUP
        for n, (b_ref, w_ref) in enumerate(((qb_ref, q4_ref), (kb_ref, k4_ref), (vb_ref, v4_ref))):
            stage_ref[n] = b_ref[0, src, :].astype(F32)
            for e in range(REGROUP):
                dst = pl.ds(pl.multiple_of(e * quarter + i * per_class, per_class), per_class)
                w_ref[dst, :] = stage_ref[n, pl.ds(e, per_class, stride=REGROUP), :]
        return carry

    lax.fori_loop(0, seq // WIDEN_ROWS, widen, 0)
    head0 = lax.broadcasted_iota(jnp.int32, (BAND, LANES), 1) < HEAD_DIM
    ii = lax.broadcasted_iota(jnp.int32, (2 * BAND, 2 * BAND), 0) % BAND
    jj = lax.broadcasted_iota(jnp.int32, (2 * BAND, 2 * BAND), 1)
    in_window = (jj <= ii + BAND) & (jj >= ii)
    bias_ref[1] = jnp.where(in_window, 0.0, -jnp.inf)
    bias_ref[0] = jnp.where(in_window & (jj >= BAND), 0.0, -jnp.inf)
    ones = jnp.ones((2 * BAND, LANES), BF16)

    def unit(c, dil, cls, m0, t):
        mp = jnp.maximum(m0 - BAND, 0)

        def block(b_ref, w_ref, start):
            if dil == 1:
                return b_ref[0, pl.ds(pl.multiple_of(start, BAND), BAND), :]
            if dil == REGROUP:
                rows = pl.ds(pl.multiple_of(cls * quarter + start, BAND), BAND)
            else:
                sub = cls // REGROUP
                step = dil // REGROUP
                rows = pl.ds((cls - sub * REGROUP) * quarter + sub + step * start, BAND, stride=step)
            return w_ref[rows, :].astype(BF16)

        q = block(qb_ref, q4_ref, m0)
        kk = jnp.concatenate([block(kb_ref, k4_ref, mp), block(kb_ref, k4_ref, m0)], axis=0)
        vv = jnp.concatenate([block(vb_ref, v4_ref, mp), block(vb_ref, v4_ref, m0)], axis=0)
        vv1 = jnp.concatenate([vv, ones], axis=1)
        zero = jnp.zeros_like(q)
        qs = jnp.concatenate([jnp.where(head0, q, zero), jnp.where(head0, zero, q)], axis=0)
        s = lax.dot_general(qs, kk, (((1,), (1,)), ((), ())), preferred_element_type=F32)
        s = s + bias_ref[jnp.where(m0 > 0, 1, 0)]
        m = jnp.max(s, axis=1, keepdims=True)
        p = jnp.exp2(s - m)
        pv = jnp.dot(p.astype(BF16), vv1, preferred_element_type=F32)
        o = jnp.where(head0, pv[:BAND, :LANES], pv[BAND:, :LANES])
        l = jnp.where(head0, pv[:BAND, LANES:], pv[BAND:, LANES:])
        mm = jnp.where(head0, jnp.broadcast_to(m[:BAND], (BAND, LANES)),
                       jnp.broadcast_to(m[BAND:], (BAND, LANES)))
        row0 = cls + dil * m0 - t * ATT_TILE
        osc[c, pl.ds(row0, BAND, stride=dil), :] = o / l
        lsc[c, pl.ds(row0, BAND, stride=dil), :] = mm + jnp.log2(l)

    units = ATT_TILE // BAND

    def tile_body(t, carry):
        for c, (_, dil) in enumerate(CONFIGS):
            blocks = units // dil

            def body(u, carry2, c=c, dil=dil, blocks=blocks):
                cls = u // blocks
                nb = u - cls * blocks
                unit(c, dil, cls, t * (ATT_TILE // dil) + nb * BAND, t)
                return carry2

            lax.fori_loop(0, units, body, 0, unroll=UNIT_UNROLL)

        rows_per_step = 256
        for r in range(ATT_TILE // rows_per_step):
            sl = pl.ds(r * rows_per_step, rows_per_step)
            ls = [lsc[c, sl, :] for c in range(len(CONFIGS))]
            mx = jnp.maximum(jnp.maximum(ls[0], ls[1]), ls[2])
            ws = [jnp.exp2(l - mx) for l in ls]
            num = ws[0] * osc[0, sl, :] + ws[1] * osc[1, sl, :] + ws[2] * osc[2, sl, :]
            dst = pl.ds(pl.multiple_of(t * ATT_TILE + r * rows_per_step, rows_per_step), rows_per_step)
            o_ref[0, dst, :] = (num / (ws[0] + ws[1] + ws[2])).astype(o_ref.dtype)
        return carry

    lax.fori_loop(0, seq // ATT_TILE, tile_body, 0)


def _attn(q, k, v):
    bn, seq, _ = q.shape
    spec = pl.BlockSpec((1, seq, LANES), lambda b, s: (b, 0, s))
    return pl.pallas_call(
        _attn_kernel,
        grid=(bn, N_SLABS),
        in_specs=[spec, spec, spec],
        out_specs=spec,
        out_shape=jax.ShapeDtypeStruct((bn, seq, B_WIDTH), BF16),
        scratch_shapes=[pltpu.VMEM((seq, LANES), F32)] * 3 + [
            pltpu.VMEM((3, WIDEN_ROWS, LANES), F32),
            pltpu.VMEM((len(CONFIGS), ATT_TILE, LANES), F32),
            pltpu.VMEM((len(CONFIGS), ATT_TILE, LANES), F32),
            pltpu.VMEM((2, 2 * BAND, 2 * BAND), F32)],
        compiler_params=pltpu.CompilerParams(
            dimension_semantics=("arbitrary", "arbitrary"), vmem_limit_bytes=VMEM_LIMIT),
        name="attn",
    )(q, k, v)


NEW_PAD = 128


def _sample_counts(w_buf, t_new):
    tq = np.arange(t_new)[:, None]

    def count(idx, exists):
        dist = w_buf + tq - idx[None, :]
        cnt = np.zeros(dist.shape, np.float64)
        for win, dil in CONFIGS:
            cnt += ((dist >= 0) & (dist <= win) & (dist % dil == 0) & exists[None, :])
        with np.errstate(divide="ignore"):
            bias = np.log2(cnt).astype(np.float32)
        return np.tile(bias, (HEADS, 1))

    old = np.arange(w_buf)
    new = np.arange(w_buf, w_buf + NEW_PAD)
    return count(old, old >= 0), count(new, new < w_buf + t_new)


def _sample_attention(q_ref, kn_ref, vn_ref, kt_ref, vt_ref, cnt_old_ref, cnt_new_ref, o_ref):
    t_new = q_ref.shape[0]
    rows = HEADS * t_new
    qrep = jnp.concatenate([q_ref[...]] * HEADS, axis=0)
    rh = lax.broadcasted_iota(jnp.int32, (rows, B_WIDTH), 0) // t_new
    lh = lax.broadcasted_iota(jnp.int32, (rows, B_WIDTH), 1) // HEAD_DIM
    own = rh == lh
    qbd = jnp.where(own, qrep, 0.0).astype(BF16)
    pad = jnp.zeros((NEW_PAD - t_new, B_WIDTH), F32)
    kn = jnp.concatenate([kn_ref[...], pad], axis=0).astype(BF16)
    vn = jnp.concatenate([vn_ref[...], pad], axis=0).astype(BF16)
    s_old = jnp.dot(qbd, kt_ref[...].astype(BF16), preferred_element_type=F32)
    s_new = lax.dot_general(qbd, kn, (((1,), (1,)), ((), ())), preferred_element_type=F32)
    s_old = s_old + cnt_old_ref[...]
    s_new = s_new + cnt_new_ref[...]
    m = jnp.maximum(jnp.max(s_old, axis=1, keepdims=True), jnp.max(s_new, axis=1, keepdims=True))
    p_old = jnp.exp2(s_old - m)
    p_new = jnp.exp2(s_new - m)
    l = jnp.sum(p_old, axis=1, keepdims=True) + jnp.sum(p_new, axis=1, keepdims=True)
    o = lax.dot_general(p_old.astype(BF16), vt_ref[...].astype(BF16), (((1,), (1,)), ((), ())),
                        preferred_element_type=F32)
    o = (o + jnp.dot(p_new.astype(BF16), vn, preferred_element_type=F32)) / l
    o = jnp.where(own, o, 0.0)
    acc = o[0:t_new]
    for h in range(1, HEADS):
        acc = acc + o[h * t_new:(h + 1) * t_new]
    o_ref[...] = acc.astype(o_ref.dtype)


def _transpose_kernel(k_ref, v_ref, kt_ref, vt_ref):
    kt_ref[0] = k_ref[...].T
    vt_ref[0] = v_ref[...].T


def _transpose_blocks(k, v, grid, src_block, src_map, out_dims, dst_map):
    src = pl.BlockSpec(src_block, src_map)
    dst = pl.BlockSpec((1, src_block[1], src_block[0]), dst_map)
    shape = jax.ShapeDtypeStruct(out_dims, F32)
    return pl.pallas_call(
        _transpose_kernel, grid=grid, in_specs=[src, src], out_specs=[dst, dst],
        out_shape=[shape, shape],
        compiler_params=pltpu.CompilerParams(dimension_semantics=("arbitrary",) * len(grid)),
        name="transpose",
    )(k, v)


def _finish_kernel(x_hbm, ha_ref, b_ref, sgb_ref, gnb_ref, wout_ref, fg_ref, y_ref, xbuf, sem, *, n_steps):
    tm = y_ref.shape[0]
    step = pl.program_id(0)

    def x_copies(j):
        slot = j % RING_SLOTS
        rows = pl.ds(pl.multiple_of(j * tm, tm), tm)
        return (pltpu.make_async_copy(x_hbm.at[rows], xbuf.at[slot], sem.at[slot]),)

    _ring_advance(x_copies, step, n_steps)
    hb = _rms(b_ref[...].astype(F32) * sgb_ref[...].astype(F32), gnb_ref[...])
    hcat = jnp.concatenate([ha_ref[...], hb.astype(BF16)], axis=1)
    y = xbuf[step % RING_SLOTS] + jnp.dot(hcat, wout_ref[...], preferred_element_type=F32)
    y_ref[...] = _rms(y, fg_ref[...])


def _finish(x, ha, b, sgb, gn_b, wout_bf, final_g, tm):
    n = x.shape[0]
    n_steps = n // tm
    row = lambda i: (i, 0)
    fixed = lambda i: (0, 0)
    half = pl.BlockSpec((tm, A_WIDTH), row)
    return pl.pallas_call(
        functools.partial(_finish_kernel, n_steps=n_steps),
        grid=(n_steps,),
        in_specs=[
            pl.BlockSpec(memory_space=pl.ANY), half, half, half,
            pl.BlockSpec((1, B_WIDTH), fixed),
            pl.BlockSpec((D_MODEL, D_MODEL), fixed),
            pl.BlockSpec((1, D_MODEL), fixed),
        ],
        out_specs=pl.BlockSpec((tm, D_MODEL), row),
        out_shape=jax.ShapeDtypeStruct((n, D_MODEL), F32),
        scratch_shapes=[pltpu.VMEM((RING_SLOTS, tm, D_MODEL), F32), pltpu.SemaphoreType.DMA((RING_SLOTS,))],
        compiler_params=pltpu.CompilerParams(
            dimension_semantics=("arbitrary",), vmem_limit_bytes=VMEM_LIMIT),
        name="finish",
    )(x, ha, b, sgb, gn_b, wout_bf, final_g)


def kernel(x_prompt, x_sample, cache_k_win, cache_v_win, norm_g, w_in, ln_v_g, ln_v_b,
           w_spatial, b_spatial, gn_a, gn_b, w_out, final_g):
    bp, seq, _ = x_prompt.shape
    bd, t_new, _ = x_sample.shape
    past_len = seq
    w_in_bf = w_in.astype(BF16)
    w_out_bf = w_out.astype(BF16)
    ng = norm_g.reshape(1, D_MODEL)
    lg, lb = ln_v_g.reshape(1, A_WIDTH), ln_v_b.reshape(1, A_WIDTH)
    ga, gb = gn_a.reshape(1, A_WIDTH), gn_b.reshape(1, B_WIDTH)
    fg = final_g.reshape(1, D_MODEL)
    tm_sample = 512
    tm_finish = 1024

    xs = x_sample.reshape(bd * t_new, D_MODEL)
    tabs_s = _rope_tables(past_len + np.tile(np.arange(t_new), bd))
    seqs_per_chunk = CHUNK // t_new
    eye = jnp.eye(seqs_per_chunk, dtype=w_spatial.dtype)
    wsp_s = jax.vmap(lambda w: jnp.kron(eye, w))(w_spatial[:, :t_new, :t_new])
    bsp_s = jnp.tile(b_spatial[:, :t_new], (1, seqs_per_chunk)).reshape(A_GROUPS, CHUNK, 1)
    q_scale = LOG2E * HEAD_DIM ** -0.5
    vas, has, sgbs, qs, ks, vs = _proj_sample(xs, ng, w_in_bf, lg, lb, tabs_s, (wsp_s, bsp_s, ga),
                                              tm_sample, q_scale)

    w_buf = cache_k_win.shape[1]
    cache_kt = cache_k_win.transpose(0, 2, 3, 1).reshape(bd, B_WIDTH, w_buf)
    cache_vt = cache_v_win.transpose(0, 2, 3, 1).reshape(bd, B_WIDTH, w_buf)
    xp = x_prompt.reshape(bp * seq, D_MODEL)
    tabs_p = _rope_tables(np.arange(seq))
    w_keep = min(WIN_MAX, seq)
    bsp = b_spatial.reshape(A_GROUPS, CHUNK, 1)
    ha, sgb, q, k, v, k_win_t, v_win_t, b_s = _proj_prompt(
        xp, bp, seq, ng, w_in_bf, lg, lb, tabs_p, (w_spatial, bsp, ga), q_scale,
        w_keep, qs, ks, vs, cache_kt, cache_vt, t_new)

    b_attn = _attn(q.reshape(bp, seq, B_WIDTH), k.reshape(bp, seq, B_WIDTH),
                   v.reshape(bp, seq, B_WIDTH))
    y_prompt = _finish(xp, ha, b_attn.reshape(bp * seq, B_WIDTH), sgb, gb, w_out_bf, fg,
                       tm_finish).reshape(bp, seq, D_MODEL)
    k_win = k_win_t.reshape(bp, HEADS, HEAD_DIM, w_keep).transpose(0, 3, 1, 2)
    v_win = v_win_t.reshape(bp, HEADS, HEAD_DIM, w_keep).transpose(0, 3, 1, 2)

    k_new_t, v_new_t = _transpose_blocks(
        ks.reshape(bd, t_new * B_WIDTH), vs.reshape(bd, t_new * B_WIDTH), (t_new,),
        (bd, B_WIDTH), lambda t: (0, t), (t_new, B_WIDTH, bd), lambda t: (t, 0, 0))
    k_new = k_new_t.reshape(t_new, HEADS, HEAD_DIM, bd).transpose(3, 0, 1, 2)
    v_new = v_new_t.reshape(t_new, HEADS, HEAD_DIM, bd).transpose(3, 0, 1, 2)
    y_sample = _finish(xs, has, b_s, sgbs, gb, w_out_bf, fg, tm_finish).reshape(bd, t_new, D_MODEL)

    return (y_prompt, y_sample, k_win, v_win, k_new, v_new, vas.reshape(bd, t_new, A_WIDTH))
```

```python
import functools

import numpy as np
import jax
import jax.numpy as jnp
from jax import lax
from jax.experimental import pallas as pl
from jax.experimental.pallas import tpu as pltpu

D_MODEL = 1024
A_WIDTH = 512
B_WIDTH = 512
A_GROUPS = 4
CHUNK = 128
HEADS = 8
HEAD_DIM = 64
ROT_DIM = 16
ROPE_THETA = 500000.0
CONFIGS = ((128, 1), (512, 4), (2048, 16))
BAND = 128
WIN_MAX = 2048
EPS = 1e-6
LOG2E = 1.4426950408889634
IN_COLS = 3 * A_WIDTH + 4 * B_WIDTH

LANES = 128
HEADS_PER_SLAB = LANES // HEAD_DIM
N_SLABS = B_WIDTH // LANES
VMEM_LIMIT = 56 * 1024 * 1024

F32 = jnp.float32
BF16 = jnp.bfloat16


def _rms(x, g):
    return x * lax.rsqrt(jnp.mean(x * x, axis=-1, keepdims=True) + EPS) * g


def _silu(x):
    return x * (0.5 + 0.5 * jnp.tanh(0.5 * x))


RING_SLOTS = 3


def _ring_advance(copies, step, n_steps):
    ahead = RING_SLOTS - 1

    @pl.when(step == 0)
    def _():
        for first in range(min(ahead, n_steps)):
            for copy in copies(first):
                copy.start()

    @pl.when(step + ahead < n_steps)
    def _():
        for copy in copies(step + ahead):
            copy.start()

    for copy in copies(step):
        copy.wait()


def _project_tile(x_ref, ng_ref, w_ref, lng_ref, lnb_ref, c_ref, sn_ref, sp_ref, wsp_ref, bsp_ref,
                  gna_ref, va_ref, ha_ref, sgb_ref, q_ref, k_ref, v_ref, q_scale, emit_kv=None):
    h = _rms(x_ref[...], ng_ref[...]).astype(BF16)
    nc = x_ref.shape[0] // CHUNK

    def col(g):
        return jnp.dot(h, w_ref[:, g * A_WIDTH:(g + 1) * A_WIDTH], preferred_element_type=F32)

    def rope(z, scale):
        c, sn, sp = c_ref[...], sn_ref[...], sp_ref[...]
        outs = []
        for s in range(N_SLABS):
            zs = z[:, s * LANES:(s + 1) * LANES]
            up = pltpu.roll(zs, LANES - ROT_DIM // 2, 1)
            dn = pltpu.roll(zs, ROT_DIM // 2, 1)
            r = zs * c + up * sn + dn * sp
            outs.append(r * scale if scale != 1.0 else r)
        return jnp.concatenate(outs, axis=1)

    va = jax.nn.gelu(col(1))
    mu = jnp.mean(va, axis=-1, keepdims=True)
    vc = va - mu
    va = vc * lax.rsqrt(jnp.mean(vc * vc, axis=-1, keepdims=True) + EPS) * lng_ref[...] + lnb_ref[...]
    if va_ref is not None:
        va_ref[...] = va
    va = va.astype(BF16)
    u = jax.nn.gelu(col(0))
    ri = lax.broadcasted_iota(jnp.int32, (CHUNK, CHUNK), 0)
    ci = lax.broadcasted_iota(jnp.int32, (CHUNK, CHUNK), 1)
    a_cols = []
    for g in range(A_GROUPS):
        wm = jnp.where(ri >= ci, wsp_ref[g], 0.0).astype(BF16)
        lanes = slice(g * CHUNK, (g + 1) * CHUNK)
        rhs = jnp.concatenate([va[c * CHUNK:(c + 1) * CHUNK, lanes] for c in range(nc)], axis=1)
        s = jnp.dot(wm, rhs, preferred_element_type=F32) + bsp_ref[g]
        s = jnp.concatenate([s[:, c * CHUNK:(c + 1) * CHUNK] for c in range(nc)], axis=0)
        a_cols.append(u[:, lanes] * s)
    a = jnp.concatenate(a_cols, axis=1)
    ha_ref[...] = _rms(a * _silu(col(2)), gna_ref[...]).astype(ha_ref.dtype)
    q_ref[...] = rope(col(3), q_scale).astype(q_ref.dtype)
    k = rope(col(4), 1.0)
    v = col(5)
    k_ref[...] = k.astype(k_ref.dtype)
    v_ref[...] = v.astype(v_ref.dtype)
    if emit_kv is not None:
        emit_kv(k, v)
    sgb_ref[...] = _silu(col(6)).astype(sgb_ref.dtype)


def _proj_sample_kernel(x_ref, ng_ref, w_ref, lng_ref, lnb_ref, c_ref, sn_ref, sp_ref, wsp_ref, bsp_ref,
                        gna_ref, va_ref, ha_ref, sgb_ref, q_ref, k_ref, v_ref, *, q_scale):
    _project_tile(x_ref, ng_ref, w_ref, lng_ref, lnb_ref, c_ref, sn_ref, sp_ref, wsp_ref, bsp_ref,
                  gna_ref, va_ref, ha_ref, sgb_ref, q_ref, k_ref, v_ref, q_scale)


def _proj_prompt_kernel(x_ref, ng_ref, w_ref, lng_ref, lnb_ref, c_ref, sn_ref, sp_ref, wsp_ref, bsp_ref,
                        gna_ref, qs_ref, kn_ref, vn_ref, ktc_hbm, vtc_hbm, cnt_old_ref, cnt_new_ref,
                        ha_ref, sgb_ref, q_ref, k_ref, v_ref, kt_ref, vt_ref, bs_ref,
                        kbuf, vbuf, sem, *, q_scale, n_steps):
    step = pl.program_id(0) * pl.num_programs(1) + pl.program_id(1)

    def window_copies(seq_idx):
        slot = seq_idx % RING_SLOTS
        return (pltpu.make_async_copy(ktc_hbm.at[seq_idx], kbuf.at[slot], sem.at[0, slot]),
                pltpu.make_async_copy(vtc_hbm.at[seq_idx], vbuf.at[slot], sem.at[1, slot]))

    _ring_advance(window_copies, step, n_steps)
    slot = step % RING_SLOTS
    _sample_attention(qs_ref, kn_ref, vn_ref, kbuf.at[slot], vbuf.at[slot], cnt_old_ref, cnt_new_ref,
                      bs_ref)

    def emit_kv(k, v):
        kt_ref[0] = k.T
        vt_ref[0] = v.T

    _project_tile(x_ref, ng_ref, w_ref, lng_ref, lnb_ref, c_ref, sn_ref, sp_ref, wsp_ref, bsp_ref,
                  gna_ref, None, ha_ref, sgb_ref, q_ref, k_ref, v_ref, q_scale, emit_kv)


def _proj_in_specs(tm, row, tab, fixed):
    return [
        pl.BlockSpec((tm, D_MODEL), row),
        pl.BlockSpec((1, D_MODEL), fixed),
        pl.BlockSpec((D_MODEL, IN_COLS), fixed),
        pl.BlockSpec((1, A_WIDTH), fixed),
        pl.BlockSpec((1, A_WIDTH), fixed),
        pl.BlockSpec((tm, LANES), tab),
        pl.BlockSpec((tm, LANES), tab),
        pl.BlockSpec((tm, LANES), tab),
        pl.BlockSpec((A_GROUPS, CHUNK, CHUNK), lambda *_: (0, 0, 0)),
        pl.BlockSpec((A_GROUPS, CHUNK, 1), lambda *_: (0, 0, 0)),
        pl.BlockSpec((1, A_WIDTH), fixed),
    ]


def _proj_sample(x, norm_g, w_bf, ln_g, ln_b, tabs, mix, tm, q_scale):
    n = x.shape[0]
    row = lambda i: (i, 0)
    fixed = lambda i: (0, 0)
    half = lambda dt: jax.ShapeDtypeStruct((n, A_WIDTH), dt)
    return pl.pallas_call(
        functools.partial(_proj_sample_kernel, q_scale=q_scale),
        grid=(n // tm,),
        in_specs=_proj_in_specs(tm, row, row, fixed),
        out_specs=[pl.BlockSpec((tm, A_WIDTH), row)] * 6,
        out_shape=[half(F32), half(BF16), half(BF16), half(F32), half(F32), half(F32)],
        compiler_params=pltpu.CompilerParams(
            dimension_semantics=("arbitrary",), vmem_limit_bytes=VMEM_LIMIT),
        name="proj_sample",
    )(x, norm_g, w_bf, ln_g, ln_b, *tabs, *mix)


def _proj_prompt(x, bn, seq, norm_g, w_bf, ln_g, ln_b, tabs, mix, q_scale, keep,
                 qs, k_new, v_new, cache_kt, cache_vt, t_new):
    n = bn * seq
    bd, _, w_buf = cache_kt.shape
    assert n % bd == 0, "one sample sequence per prompt tile"
    tm = n // bd
    assert tm % CHUNK == 0 and seq % tm == 0 and keep % tm == 0
    per_seq = seq // tm
    first = (seq - keep) // tm
    cnt_old, cnt_new = (jnp.asarray(c) for c in _sample_counts(w_buf, t_new))
    row = lambda b, j: (b * per_seq + j, 0)
    tab = lambda b, j: (j, 0)
    fixed = lambda b, j: (0, 0)
    new_spec = pl.BlockSpec((t_new, B_WIDTH), row)
    ring = pltpu.VMEM((RING_SLOTS, B_WIDTH, w_buf), F32)
    win_spec = pl.BlockSpec((1, B_WIDTH, tm), lambda b, j: (b, 0, jnp.maximum(j - first, 0)))
    half = lambda dt: jax.ShapeDtypeStruct((n, A_WIDTH), dt)
    win = jax.ShapeDtypeStruct((bn, B_WIDTH, keep), F32)
    return pl.pallas_call(
        functools.partial(_proj_prompt_kernel, q_scale=q_scale, n_steps=bd),
        grid=(bn, per_seq),
        in_specs=_proj_in_specs(tm, row, tab, fixed) + [
            new_spec, new_spec, new_spec,
            pl.BlockSpec(memory_space=pl.ANY), pl.BlockSpec(memory_space=pl.ANY),
            pl.BlockSpec(cnt_old.shape, fixed), pl.BlockSpec(cnt_new.shape, fixed)],
        out_specs=[pl.BlockSpec((tm, A_WIDTH), row)] * 5 + [win_spec, win_spec, new_spec],
        out_shape=[half(BF16)] * 5 + [win, win, jax.ShapeDtypeStruct((bd * t_new, B_WIDTH), BF16)],
        scratch_shapes=[ring, ring, pltpu.SemaphoreType.DMA((2, RING_SLOTS))],
        compiler_params=pltpu.CompilerParams(
            dimension_semantics=("arbitrary", "arbitrary"), vmem_limit_bytes=VMEM_LIMIT),
        name="proj_prompt",
    )(x, norm_g, w_bf, ln_g, ln_b, *tabs, *mix, qs, k_new, v_new, cache_kt, cache_vt, cnt_old, cnt_new)


def _rope_tables(pos):
    pos = np.asarray(pos, np.float64)
    inv = ROPE_THETA ** (-np.arange(0, ROT_DIM, 2, dtype=np.float64) / ROT_DIM)
    ang = pos[:, None] * inv[None, :]
    cos, sin = np.cos(ang), np.sin(ang)
    half = ROT_DIM // 2
    ones = np.ones((pos.shape[0], HEAD_DIM - ROT_DIM))
    zeros = np.zeros((pos.shape[0], HEAD_DIM - half))
    c = np.concatenate([cos, cos, ones], axis=1)
    sn = np.concatenate([-sin, zeros], axis=1)
    sp = np.concatenate([zeros[:, :half], sin, zeros[:, :HEAD_DIM - ROT_DIM]], axis=1)
    rep = lambda t: jnp.asarray(np.tile(t, (1, HEADS_PER_SLAB)).astype(np.float32))
    return rep(c), rep(sn), rep(sp)


ATT_TILE = WIN_MAX
UNIT_UNROLL = 16
WIDEN_ROWS = 512
REGROUP = 4


def _attn_kernel(qb_ref, kb_ref, vb_ref, o_ref, q4_ref, k4_ref, v4_ref, stage_ref, osc, lsc, bias_ref):
    seq = qb_ref.shape[1]
    quarter = seq // REGROUP

    def widen(i, carry):
        src = pl.ds(pl.multiple_of(i * WIDEN_ROWS, WIDEN_ROWS), WIDEN_ROWS)
        per_class = WIDEN_ROWS // REGROUP
        for n, (b_ref, w_ref) in enumerate(((qb_ref, q4_ref), (kb_ref, k4_ref), (vb_ref, v4_ref))):
            stage_ref[n] = b_ref[0, src, :].astype(F32)
            for e in range(REGROUP):
                dst = pl.ds(pl.multiple_of(e * quarter + i * per_class, per_class), per_class)
                w_ref[dst, :] = stage_ref[n, pl.ds(e, per_class, stride=REGROUP), :]
        return carry

    lax.fori_loop(0, seq // WIDEN_ROWS, widen, 0)
    head0 = lax.broadcasted_iota(jnp.int32, (BAND, LANES), 1) < HEAD_DIM
    ii = lax.broadcasted_iota(jnp.int32, (2 * BAND, 2 * BAND), 0) % BAND
    jj = lax.broadcasted_iota(jnp.int32, (2 * BAND, 2 * BAND), 1)
    in_window = (jj <= ii + BAND) & (jj >= ii)
    bias_ref[1] = jnp.where(in_window, 0.0, -jnp.inf)
    bias_ref[0] = jnp.where(in_window & (jj >= BAND), 0.0, -jnp.inf)
    ones = jnp.ones((2 * BAND, LANES), BF16)

    def unit(c, dil, cls, m0, t):
        mp = jnp.maximum(m0 - BAND, 0)

        def block(b_ref, w_ref, start):
            if dil == 1:
                return b_ref[0, pl.ds(pl.multiple_of(start, BAND), BAND), :]
            if dil == REGROUP:
                rows = pl.ds(pl.multiple_of(cls * quarter + start, BAND), BAND)
            else:
                sub = cls // REGROUP
                step = dil // REGROUP
                rows = pl.ds((cls - sub * REGROUP) * quarter + sub + step * start, BAND, stride=step)
            return w_ref[rows, :].astype(BF16)

        q = block(qb_ref, q4_ref, m0)
        kk = jnp.concatenate([block(kb_ref, k4_ref, mp), block(kb_ref, k4_ref, m0)], axis=0)
        vv = jnp.concatenate([block(vb_ref, v4_ref, mp), block(vb_ref, v4_ref, m0)], axis=0)
        vv1 = jnp.concatenate([vv, ones], axis=1)
        zero = jnp.zeros_like(q)
        qs = jnp.concatenate([jnp.where(head0, q, zero), jnp.where(head0, zero, q)], axis=0)
        s = lax.dot_general(qs, kk, (((1,), (1,)), ((), ())), preferred_element_type=F32)
        s = s + bias_ref[jnp.where(m0 > 0, 1, 0)]
        m = jnp.max(s, axis=1, keepdims=True)
        p = jnp.exp2(s - m)
        pv = jnp.dot(p.astype(BF16), vv1, preferred_element_type=F32)
        o = jnp.where(head0, pv[:BAND, :LANES], pv[BAND:, :LANES])
        l = jnp.where(head0, pv[:BAND, LANES:], pv[BAND:, LANES:])
        mm = jnp.where(head0, jnp.broadcast_to(m[:BAND], (BAND, LANES)),
                       jnp.broadcast_to(m[BAND:], (BAND, LANES)))
        row0 = cls + dil * m0 - t * ATT_TILE
        osc[c, pl.ds(row0, BAND, stride=dil), :] = o / l
        lsc[c, pl.ds(row0, BAND, stride=dil), :] = mm + jnp.log2(l)

    units = ATT_TILE // BAND

    def tile_body(t, carry):
        for c, (_, dil) in enumerate(CONFIGS):
            blocks = units // dil

            def body(u, carry2, c=c, dil=dil, blocks=blocks):
                cls = u // blocks
                nb = u - cls * blocks
                unit(c, dil, cls, t * (ATT_TILE // dil) + nb * BAND, t)
                return carry2

            lax.fori_loop(0, units, body, 0, unroll=UNIT_UNROLL)

        rows_per_step = 256
        for r in range(ATT_TILE // rows_per_step):
            sl = pl.ds(r * rows_per_step, rows_per_step)
            ls = [lsc[c, sl, :] for c in range(len(CONFIGS))]
            mx = jnp.maximum(jnp.maximum(ls[0], ls[1]), ls[2])
            ws = [jnp.exp2(l - mx) for l in ls]
            num = ws[0] * osc[0, sl, :] + ws[1] * osc[1, sl, :] + ws[2] * osc[2, sl, :]
            dst = pl.ds(pl.multiple_of(t * ATT_TILE + r * rows_per_step, rows_per_step), rows_per_step)
            o_ref[0, dst, :] = (num / (ws[0] + ws[1] + ws[2])).astype(o_ref.dtype)
        return carry

    lax.fori_loop(0, seq // ATT_TILE, tile_body, 0)


def _attn(q, k, v):
    bn, seq, _ = q.shape
    spec = pl.BlockSpec((1, seq, LANES), lambda b, s: (b, 0, s))
    return pl.pallas_call(
        _attn_kernel,
        grid=(bn, N_SLABS),
        in_specs=[spec, spec, spec],
        out_specs=spec,
        out_shape=jax.ShapeDtypeStruct((bn, seq, B_WIDTH), BF16),
        scratch_shapes=[pltpu.VMEM((seq, LANES), F32)] * 3 + [
            pltpu.VMEM((3, WIDEN_ROWS, LANES), F32),
            pltpu.VMEM((len(CONFIGS), ATT_TILE, LANES), F32),
            pltpu.VMEM((len(CONFIGS), ATT_TILE, LANES), F32),
            pltpu.VMEM((2, 2 * BAND, 2 * BAND), F32)],
        compiler_params=pltpu.CompilerParams(
            dimension_semantics=("arbitrary", "arbitrary"), vmem_limit_bytes=VMEM_LIMIT),
        name="attn",
    )(q, k, v)


NEW_PAD = 128


def _sample_counts(w_buf, t_new):
    tq = np.arange(t_new)[:, None]

    def count(idx, exists):
        dist = w_buf + tq - idx[None, :]
        cnt = np.zeros(dist.shape, np.float64)
        for win, dil in CONFIGS:
            cnt += ((dist >= 0) & (dist <= win) & (dist % dil == 0) & exists[None, :])
        with np.errstate(divide="ignore"):
            bias = np.log2(cnt).astype(np.float32)
        return np.tile(bias, (HEADS, 1))

    old = np.arange(w_buf)
    new = np.arange(w_buf, w_buf + NEW_PAD)
    return count(old, old >= 0), count(new, new < w_buf + t_new)


def _sample_attention(q_ref, kn_ref, vn_ref, kt_ref, vt_ref, cnt_old_ref, cnt_new_ref, o_ref):
    t_new = q_ref.shape[0]
    rows = HEADS * t_new
    qrep = jnp.concatenate([q_ref[...]] * HEADS, axis=0)
    rh = lax.broadcasted_iota(jnp.int32, (rows, B_WIDTH), 0) // t_new
    lh = lax.broadcasted_iota(jnp.int32, (rows, B_WIDTH), 1) // HEAD_DIM
    own = rh == lh
    qbd = jnp.where(own, qrep, 0.0).astype(BF16)
    pad = jnp.zeros((NEW_PAD - t_new, B_WIDTH), F32)
    kn = jnp.concatenate([kn_ref[...], pad], axis=0).astype(BF16)
    vn = jnp.concatenate([vn_ref[...], pad], axis=0).astype(BF16)
    s_old = jnp.dot(qbd, kt_ref[...].astype(BF16), preferred_element_type=F32)
    s_new = lax.dot_general(qbd, kn, (((1,), (1,)), ((), ())), preferred_element_type=F32)
    s_old = s_old + cnt_old_ref[...]
    s_new = s_new + cnt_new_ref[...]
    m = jnp.maximum(jnp.max(s_old, axis=1, keepdims=True), jnp.max(s_new, axis=1, keepdims=True))
    p_old = jnp.exp2(s_old - m)
    p_new = jnp.exp2(s_new - m)
    l = jnp.sum(p_old, axis=1, keepdims=True) + jnp.sum(p_new, axis=1, keepdims=True)
    o = lax.dot_general(p_old.astype(BF16), vt_ref[...].astype(BF16), (((1,), (1,)), ((), ())),
                        preferred_element_type=F32)
    o = (o + jnp.dot(p_new.astype(BF16), vn, preferred_element_type=F32)) / l
    o = jnp.where(own, o, 0.0)
    acc = o[0:t_new]
    for h in range(1, HEADS):
        acc = acc + o[h * t_new:(h + 1) * t_new]
    o_ref[...] = acc.astype(o_ref.dtype)


def _transpose_kernel(k_ref, v_ref, kt_ref, vt_ref):
    kt_ref[0] = k_ref[...].T
    vt_ref[0] = v_ref[...].T


def _transpose_blocks(k, v, grid, src_block, src_map, out_dims, dst_map):
    src = pl.BlockSpec(src_block, src_map)
    dst = pl.BlockSpec((1, src_block[1], src_block[0]), dst_map)
    shape = jax.ShapeDtypeStruct(out_dims, F32)
    return pl.pallas_call(
        _transpose_kernel, grid=grid, in_specs=[src, src], out_specs=[dst, dst],
        out_shape=[shape, shape],
        compiler_params=pltpu.CompilerParams(dimension_semantics=("arbitrary",) * len(grid)),
        name="transpose",
    )(k, v)


def _finish_kernel(x_hbm, ha_ref, b_ref, sgb_ref, gnb_ref, wout_ref, fg_ref, y_ref, xbuf, sem, *, n_steps):
    tm = y_ref.shape[0]
    step = pl.program_id(0)

    def x_copies(j):
        slot = j % RING_SLOTS
        rows = pl.ds(pl.multiple_of(j * tm, tm), tm)
        return (pltpu.make_async_copy(x_hbm.at[rows], xbuf.at[slot], sem.at[slot]),)

    _ring_advance(x_copies, step, n_steps)
    hb = _rms(b_ref[...].astype(F32) * sgb_ref[...].astype(F32), gnb_ref[...])
    hcat = jnp.concatenate([ha_ref[...], hb.astype(BF16)], axis=1)
    y = xbuf[step % RING_SLOTS] + jnp.dot(hcat, wout_ref[...], preferred_element_type=F32)
    y_ref[...] = _rms(y, fg_ref[...])


def _finish(x, ha, b, sgb, gn_b, wout_bf, final_g, tm):
    n = x.shape[0]
    n_steps = n // tm
    row = lambda i: (i, 0)
    fixed = lambda i: (0, 0)
    half = pl.BlockSpec((tm, A_WIDTH), row)
    return pl.pallas_call(
        functools.partial(_finish_kernel, n_steps=n_steps),
        grid=(n_steps,),
        in_specs=[
            pl.BlockSpec(memory_space=pl.ANY), half, half, half,
            pl.BlockSpec((1, B_WIDTH), fixed),
            pl.BlockSpec((D_MODEL, D_MODEL), fixed),
            pl.BlockSpec((1, D_MODEL), fixed),
        ],
        out_specs=pl.BlockSpec((tm, D_MODEL), row),
        out_shape=jax.ShapeDtypeStruct((n, D_MODEL), F32),
        scratch_shapes=[pltpu.VMEM((RING_SLOTS, tm, D_MODEL), F32), pltpu.SemaphoreType.DMA((RING_SLOTS,))],
        compiler_params=pltpu.CompilerParams(
            dimension_semantics=("arbitrary",), vmem_limit_bytes=VMEM_LIMIT),
        name="finish",
    )(x, ha, b, sgb, gn_b, wout_bf, final_g)


def kernel(x_prompt, x_sample, cache_k_win, cache_v_win, norm_g, w_in, ln_v_g, ln_v_b,
           w_spatial, b_spatial, gn_a, gn_b, w_out, final_g):
    bp, seq, _ = x_prompt.shape
    bd, t_new, _ = x_sample.shape
    past_len = seq
    w_in_bf = w_in.astype(BF16)
    w_out_bf = w_out.astype(BF16)
    ng = norm_g.reshape(1, D_MODEL)
    lg, lb = ln_v_g.reshape(1, A_WIDTH), ln_v_b.reshape(1, A_WIDTH)
    ga, gb = gn_a.reshape(1, A_WIDTH), gn_b.reshape(1, B_WIDTH)
    fg = final_g.reshape(1, D_MODEL)
    tm_sample = 512
    tm_finish = 1024

    xs = x_sample.reshape(bd * t_new, D_MODEL)
    tabs_s = _rope_tables(past_len + np.tile(np.arange(t_new), bd))
    seqs_per_chunk = CHUNK // t_new
    eye = jnp.eye(seqs_per_chunk, dtype=w_spatial.dtype)
    wsp_s = jax.vmap(lambda w: jnp.kron(eye, w))(w_spatial[:, :t_new, :t_new])
    bsp_s = jnp.tile(b_spatial[:, :t_new], (1, seqs_per_chunk)).reshape(A_GROUPS, CHUNK, 1)
    q_scale = LOG2E * HEAD_DIM ** -0.5
    vas, has, sgbs, qs, ks, vs = _proj_sample(xs, ng, w_in_bf, lg, lb, tabs_s, (wsp_s, bsp_s, ga),
                                              tm_sample, q_scale)

    w_buf = cache_k_win.shape[1]
    cache_kt = cache_k_win.transpose(0, 2, 3, 1).reshape(bd, B_WIDTH, w_buf)
    cache_vt = cache_v_win.transpose(0, 2, 3, 1).reshape(bd, B_WIDTH, w_buf)
    xp = x_prompt.reshape(bp * seq, D_MODEL)
    tabs_p = _rope_tables(np.arange(seq))
    w_keep = min(WIN_MAX, seq)
    bsp = b_spatial.reshape(A_GROUPS, CHUNK, 1)
    ha, sgb, q, k, v, k_win_t, v_win_t, b_s = _proj_prompt(
        xp, bp, seq, ng, w_in_bf, lg, lb, tabs_p, (w_spatial, bsp, ga), q_scale,
        w_keep, qs, ks, vs, cache_kt, cache_vt, t_new)

    b_attn = _attn(q.reshape(bp, seq, B_WIDTH), k.reshape(bp, seq, B_WIDTH),
                   v.reshape(bp, seq, B_WIDTH))
    y_prompt = _finish(xp, ha, b_attn.reshape(bp * seq, B_WIDTH), sgb, gb, w_out_bf, fg,
                       tm_finish).reshape(bp, seq, D_MODEL)
    k_win = k_win_t.reshape(bp, HEADS, HEAD_DIM, w_keep).transpose(0, 3, 1, 2)
    v_win = v_win_t.reshape(bp, HEADS, HEAD_DIM, w_keep).transpose(0, 3, 1, 2)

    k_new_t, v_new_t = _transpose_blocks(
        ks.reshape(bd, t_new * B_WIDTH), vs.reshape(bd, t_new * B_WIDTH), (t_new,),
        (bd, B_WIDTH), lambda t: (0, t), (t_new, B_WIDTH, bd), lambda t: (t, 0, 0))
    k_new = k_new_t.reshape(t_new, HEADS, HEAD_DIM, bd).transpose(3, 0, 1, 2)
    v_new = v_new_t.reshape(t_new, HEADS, HEAD_DIM, bd).transpose(3, 0, 1, 2)
    y_sample = _finish(xs, has, b_s, sgbs, gb, w_out_bf, fg, tm_finish).reshape(bd, t_new, D_MODEL)

    return (y_prompt, y_sample, k_win, v_win, k_new, v_new, vas.reshape(bd, t_new, A_WIDTH))
```
